```python
import math
import jax, jax.numpy as jnp
from jax import lax
import numpy as np

D_MODEL = 1024
BATCH = 2
SEQ = 16384
DEPTH = 4

F32 = jnp.float32
GRID_W = 64
CTX_LEN = 256
HEAD_DIM = 64
NA_HEADS = 4
WIN_H = 8
WIN_W = 16
DIFF_HEADS = 4
DIFF_DK = 32
DIFF_DV = 64
MLA_HEADS = 4
MLA_Q_RANK = 256
MLA_KV_RANK = 128
MLA_NOPE = 64
MLA_ROPE = 32
MLA_V = 64
S5_GROUPS = 16
S5_GROUP_CH = 16
S5_STATE = 64
S5_WIDTH = S5_GROUPS * S5_GROUP_CH
N_BRANCH = 4
BRANCH_W = 256
N_EXPERTS = 64
TOP_K = 6
EXPERT_FF = 256
ROUTED_SCALE = 1.0
EXPERT_BLOCK = 256
Q_BLOCK = 128
ROPE_BASE = 10000.0
EPS = 1e-6
NEG_INF = -1e30

NA_W = NA_HEADS * HEAD_DIM
DIFF_QK_W = DIFF_HEADS * 2 * DIFF_DK
DIFF_V_W = DIFF_HEADS * DIFF_DV
IN_SIZES = (3 * NA_W, DIFF_QK_W, DIFF_QK_W, DIFF_V_W, MLA_Q_RANK, MLA_KV_RANK, MLA_ROPE, S5_WIDTH, N_BRANCH * D_MODEL)
IN_W = sum(IN_SIZES)
IN_OFFSETS = tuple(int(v) for v in np.cumsum(IN_SIZES)[:-1])

kernel_name = "hybrid_natten_diff_mla_s5_moe_dit"


def rmsnorm(x, g):
    xf = x.astype(F32)
    y = xf * lax.rsqrt(jnp.mean(xf * xf, axis=-1, keepdims=True) + EPS)
    return (y * g.astype(F32)).astype(x.dtype)


def modulate(h, shift, scale):
    return h * (1 + scale) + shift


def heads_to_tokens(o):
    b, h, l, d = o.shape
    return o.transpose(0, 2, 1, 3).reshape(b, l, h * d)


def _rotate(x, pos):
    n = x.shape[-1]
    half = n // 2
    freqs = ROPE_BASE ** (-jnp.arange(half, dtype=F32) * 2.0 / n)
    ang = pos.astype(F32)[:, None] * freqs[None, :]
    cos, sin = jnp.cos(ang), jnp.sin(ang)
    xf = x.astype(F32)
    x1, x2 = xf[..., :half], xf[..., half:]
    return jnp.concatenate([x1 * cos - x2 * sin, x1 * sin + x2 * cos], axis=-1).astype(x.dtype)


def rope_2d(x, rows, cols):
    r = x.shape[-1] // 2
    return jnp.concatenate([_rotate(x[..., :r], rows), _rotate(x[..., r:], cols)], axis=-1)


def ctx_attention(q, k, v, scale):
    s = jnp.einsum('bhqd,bhkd->bhqk', q, k).astype(F32) * scale
    p = jax.nn.softmax(s, axis=-1).astype(v.dtype)
    return jnp.einsum('bhqk,bhkd->bhqd', p, v)


def attend_latent(q_lat, q_ctx, k_lat, k_ctx, v_lat, v_ctx, scale):
    b, h, s, _ = q_lat.shape
    nb = s // Q_BLOCK

    def to_blocks(q):
        return q.reshape(b, h, nb, Q_BLOCK, q.shape[-1]).transpose(2, 0, 1, 3, 4)

    def one_block(qs):
        ql, qc = qs
        s_lat = jnp.einsum('bhqd,bhkd->bhqk', ql, k_lat).astype(F32)
        s_ctx = jnp.einsum('bhqd,bhkd->bhqk', qc, k_ctx).astype(F32)
        p = jax.nn.softmax(jnp.concatenate([s_lat, s_ctx], axis=-1) * scale, axis=-1).astype(v_lat.dtype)
        return (jnp.einsum('bhqk,bhkd->bhqd', p[..., :s], v_lat)
                + jnp.einsum('bhqk,bhkd->bhqd', p[..., s:], v_ctx))

    o = lax.map(one_block, (to_blocks(q_lat), to_blocks(q_ctx)))
    return o.transpose(1, 2, 0, 3, 4).reshape(b, h, s, -1)


def neighbourhood_attention(qkv_l, qkv_c, rpb, with_ctx_out):
    b, s, _ = qkv_l.shape
    lc = qkv_c.shape[1]
    n_rows = s // GRID_W
    kh = min(WIN_H, n_rows)
    q, k, v = qkv_l.reshape(b, s, 3, NA_HEADS, HEAD_DIM).transpose(2, 0, 3, 1, 4)
    qc, kc, vc = qkv_c.reshape(b, lc, 3, NA_HEADS, HEAD_DIM).transpose(2, 0, 3, 1, 4)
    scale = HEAD_DIM ** -0.5

    def grid(t):
        return t.reshape(b, NA_HEADS, n_rows, GRID_W, HEAD_DIM)

    qg, kg, vg = grid(q), grid(k), grid(v)
    r = jnp.arange(n_rows)
    ridx = jnp.clip(r - kh // 2, 0, n_rows - kh)[:, None] + jnp.arange(kh)[None, :]
    kw = kg[:, :, ridx].reshape(b, NA_HEADS, n_rows, kh * GRID_W, HEAD_DIM)
    vw = vg[:, :, ridx].reshape(b, NA_HEADS, n_rows, kh * GRID_W, HEAD_DIM)
    col = jnp.arange(GRID_W)
    cstart = jnp.clip(col - WIN_W // 2, 0, GRID_W - WIN_W)
    valid = (col[None, :] >= cstart[:, None]) & (col[None, :] < cstart[:, None] + WIN_W)
    dr = ridx - r[:, None] + (WIN_H - 1)
    dc = jnp.clip(col[None, :] - col[:, None], 1 - WIN_W, WIN_W - 1) + (WIN_W - 1)
    bias = rpb.astype(F32)[:, dr[:, None, :, None], dc[None, :, None, :]]
    bias = jnp.where(valid[:, None, :], bias, NEG_INF).reshape(NA_HEADS, n_rows, GRID_W, kh * GRID_W)
    s_win = jnp.einsum('bhrqd,bhrkd->bhrqk', qg, kw).astype(F32) * scale + bias
    s_ctx = jnp.einsum('bhrqd,bhcd->bhrqc', qg, kc).astype(F32) * scale
    p = jax.nn.softmax(jnp.concatenate([s_win, s_ctx], axis=-1), axis=-1).astype(v.dtype)
    nw = kh * GRID_W
    o = (jnp.einsum('bhrqk,bhrkd->bhrqd', p[..., :nw], vw)
         + jnp.einsum('bhrqc,bhcd->bhrqd', p[..., nw:], vc))
    y_l = heads_to_tokens(o.reshape(b, NA_HEADS, s, HEAD_DIM))
    y_c = heads_to_tokens(ctx_attention(qc, kc, vc, scale)) if with_ctx_out else None
    return y_l, y_c


def diff_attention(q_l, k_l, v_l, q_c, k_c, v_c, lam, g_sub, lam_init, rows, cols, with_ctx_out):
    b = q_l.shape[0]

    def qk_heads(t):
        return t.reshape(b, t.shape[1], DIFF_HEADS * 2, DIFF_DK).transpose(0, 2, 1, 3)

    def v_heads(t):
        vv = t.reshape(b, t.shape[1], DIFF_HEADS, DIFF_DV).transpose(0, 2, 1, 3)
        return jnp.repeat(vv, 2, axis=1)

    q, k, v = qk_heads(q_l), qk_heads(k_l), v_heads(v_l)
    kc, vc = qk_heads(k_c), v_heads(v_c)
    scale = DIFF_DK ** -0.5
    lf = lam.astype(F32)
    lam_full = jnp.exp(jnp.sum(lf[0] * lf[1])) - jnp.exp(jnp.sum(lf[2] * lf[3])) + lam_init

    def combine(o):
        l = o.shape[2]
        o5 = o.reshape(b, DIFF_HEADS, 2, l, DIFF_DV).astype(F32)
        d = rmsnorm(o5[:, :, 0] - lam_full * o5[:, :, 1], g_sub) * (1.0 - lam_init)
        return heads_to_tokens(d.astype(o.dtype))

    o = attend_latent(rope_2d(q, rows, cols), q, rope_2d(k, rows, cols), kc, v, vc, scale)
    y_l = combine(o)
    y_c = combine(ctx_attention(qk_heads(q_c), kc, vc, scale)) if with_ctx_out else None
    return y_l, y_c


def mla(cq_l, ckv_l, kpe_l, cq_c, ckv_c, kpe_c, g_q, g_kv, w_uq, w_ukv, rows, cols, with_ctx_out):
    b = cq_l.shape[0]

    def q_heads(cq):
        q = (rmsnorm(cq, g_q) @ w_uq).reshape(b, cq.shape[1], MLA_HEADS, MLA_NOPE + MLA_ROPE).transpose(0, 2, 1, 3)
        return q[..., :MLA_NOPE], q[..., MLA_NOPE:]

    def kv_heads(ckv):
        kv = (rmsnorm(ckv, g_kv) @ w_ukv).reshape(b, ckv.shape[1], MLA_HEADS, MLA_NOPE + MLA_V).transpose(0, 2, 1, 3)
        return kv[..., :MLA_NOPE], kv[..., MLA_NOPE:]

    def with_rope_key(k_nope, k_pe):
        return jnp.concatenate([k_nope, jnp.broadcast_to(k_pe[:, None], k_nope.shape[:3] + (MLA_ROPE,))], axis=-1)

    qn, qp = q_heads(cq_l)
    kn, v = kv_heads(ckv_l)
    knc, vc = kv_heads(ckv_c)
    k_ctx = with_rope_key(knc, kpe_c)
    k_lat = with_rope_key(kn, rope_2d(kpe_l, rows, cols))
    q_lat = jnp.concatenate([qn, rope_2d(qp, rows, cols)], axis=-1)
    q_lc = jnp.concatenate([qn, qp], axis=-1)
    scale = (MLA_NOPE + MLA_ROPE) ** -0.5
    y_l = heads_to_tokens(attend_latent(q_lat, q_lc, k_lat, k_ctx, v, vc, scale))
    y_c = None
    if with_ctx_out:
        qnc, qpc = q_heads(cq_c)
        y_c = heads_to_tokens(ctx_attention(jnp.concatenate([qnc, qpc], axis=-1), k_ctx, vc, scale))
    return y_l, y_c


def _cplx_combine(e1, e2):
    a1r, a1i, b1r, b1i = e1
    a2r, a2i, b2r, b2i = e2
    return (a2r * a1r - a2i * a1i, a2r * a1i + a2i * a1r,
            a2r * b1r - a2i * b1i + b2r, a2r * b1i + a2i * b1r + b2i)


def s5_discretise(lam_re, lam_im, log_step, b_re, b_im):
    lr, li = lam_re.astype(F32), lam_im.astype(F32)
    step = jnp.exp(log_step.astype(F32))[:, None]
    mag = jnp.exp(lr * step)
    ar, ai = mag * jnp.cos(li * step), mag * jnp.sin(li * step)
    den = lr * lr + li * li
    fr = ((ar - 1.0) * lr + ai * li) / den
    fi = (ai * lr - (ar - 1.0) * li) / den
    br, bi = b_re.astype(F32), b_im.astype(F32)
    return ar, ai, fr[..., None] * br - fi[..., None] * bi, fr[..., None] * bi + fi[..., None] * br


def s5_scan(u, disc, h0, reverse):
    ar, ai, bbr, bbi = disc
    br = jnp.einsum('blgj,gpj->lbgp', u, bbr)
    bi = jnp.einsum('blgj,gpj->lbgp', u, bbi)
    if h0 is not None:
        h0r, h0i = h0
        pos = -1 if reverse else 0
        br = br.at[pos].add(ar * h0r - ai * h0i)
        bi = bi.at[pos].add(ar * h0i + ai * h0r)
    a_r = jnp.broadcast_to(ar, br.shape)
    a_i = jnp.broadcast_to(ai, bi.shape)
    _, _, hr, hi = lax.associative_scan(_cplx_combine, (a_r, a_i, br, bi), reverse=reverse, axis=0)
    return hr, hi


def s5_readout(h, c_re, c_im):
    hr, hi = h
    return (jnp.einsum('lbgp,gjp->blgj', hr, c_re.astype(F32))
            - jnp.einsum('lbgp,gjp->blgj', hi, c_im.astype(F32)))


def s5_mixer(u_l, u_c, lam_re, lam_im, log_step, b_re, b_im, c_re, c_im, d, w_glu, b_glu, with_ctx_out):
    b, s, _ = u_l.shape
    lc = u_c.shape[1]
    disc_f = s5_discretise(lam_re[0], lam_im[0], log_step[0], b_re[0], b_im[0])
    disc_b = s5_discretise(lam_re[1], lam_im[1], log_step[1], b_re[1], b_im[1])
    ul = u_l.astype(F32).reshape(b, s, S5_GROUPS, S5_GROUP_CH)
    uc = u_c.astype(F32).reshape(b, lc, S5_GROUPS, S5_GROUP_CH)
    hf_c = s5_scan(uc, disc_f, None, False)
    hb_c = s5_scan(uc, disc_b, None, True)
    hf_l = s5_scan(ul, disc_f, (hf_c[0][-1], hf_c[1][-1]), False)
    hb_l = s5_scan(ul, disc_b, (hb_c[0][0], hb_c[1][0]), True)
    d_f = d.astype(F32)

    def out(hf, hb, u, l):
        y = s5_readout(hf, c_re[0], c_im[0]) + s5_readout(hb, c_re[1], c_im[1]) + d_f * u
        g = jax.nn.gelu(y.reshape(b, l, S5_WIDTH))
        return (g * jax.nn.sigmoid(g @ w_glu.astype(F32) + b_glu.astype(F32))).astype(u_l.dtype)

    y_l = out(hf_l, hb_l, ul, s)
    y_c = out(hf_c, hb_c, uc, lc) if with_ctx_out else None
    return y_l, y_c


def hybrid_mixer(hl, hc, rows, cols, layer, with_ctx_out, w_in, na_rpb, diff_lambda, g_diff, g_mla_q, g_mla_kv,
                 w_mla_uq, w_mla_ukv, s5_lam_re, s5_lam_im, s5_log_step, s5_b_re, s5_b_im, s5_c_re, s5_c_im,
                 s5_d, w_glu, b_glu, w_branch, w_out):
    pl = jnp.split(hl @ w_in, IN_OFFSETS, axis=-1)
    pc = jnp.split(hc @ w_in, IN_OFFSETS, axis=-1)
    lam_init = 0.8 - 0.6 * math.exp(-0.3 * layer)
    ya = neighbourhood_attention(pl[0], pc[0], na_rpb, with_ctx_out)
    yb = diff_attention(pl[1], pl[2], pl[3], pc[1], pc[2], pc[3], diff_lambda, g_diff, lam_init, rows, cols,
                        with_ctx_out)
    ym = mla(pl[4], pl[5], pl[6], pc[4], pc[5], pc[6], g_mla_q, g_mla_kv, w_mla_uq, w_mla_ukv, rows, cols,
             with_ctx_out)
    yd = s5_mixer(pl[7], pc[7], s5_lam_re, s5_lam_im, s5_log_step, s5_b_re, s5_b_im, s5_c_re, s5_c_im, s5_d,
                  w_glu, b_glu, with_ctx_out)
    branches = (ya, yb, ym, yd)

    def merge(ys, gate_pre):
        b, l, _ = gate_pre.shape
        gates = jax.nn.sigmoid(gate_pre.astype(F32).reshape(b, l, N_BRANCH, D_MODEL)).astype(gate_pre.dtype)
        m = gates[:, :, 0] * (ys[0] @ w_branch[0])
        for i in range(1, N_BRANCH):
            m = m + gates[:, :, i] * (ys[i] @ w_branch[i])
        return m @ w_out

    y_l = merge([br[0] for br in branches], pl[8])
    y_c = merge([br[1] for br in branches], pc[8]) if with_ctx_out else None
    return y_l, y_c


def swiglu(x, w1, w3, w2):
    return (jax.nn.silu(x @ w1) * (x @ w3)) @ w2


def routed_experts(h, idx, gw, w_e1, w_e3, w_e2):
    n, d = h.shape
    n_assign = n * TOP_K
    e_flat = idx.reshape(-1)
    t_flat = jnp.repeat(jnp.arange(n, dtype=jnp.int32), TOP_K)
    w_flat = gw.reshape(-1)
    order = jnp.argsort(e_flat)
    e_s, t_s, w_s = e_flat[order], t_flat[order], w_flat[order]
    counts = jnp.bincount(e_flat, length=N_EXPERTS)
    starts = jnp.cumsum(counts) - counts
    padded = (counts + EXPERT_BLOCK - 1) // EXPERT_BLOCK * EXPERT_BLOCK
    pends = jnp.cumsum(padded)
    pstarts = pends - padded
    dest = pstarts[e_s] + jnp.arange(n_assign, dtype=jnp.int32) - starts[e_s]
    n_blocks = -(-n_assign // EXPERT_BLOCK) + N_EXPERTS
    tok = jnp.full((n_blocks * EXPERT_BLOCK,), n, jnp.int32).at[dest].set(t_s)
    wt = jnp.zeros((n_blocks * EXPERT_BLOCK,), h.dtype).at[dest].set(w_s)
    blk_e = jnp.minimum(jnp.searchsorted(pends, jnp.arange(n_blocks, dtype=jnp.int32) * EXPERT_BLOCK,
                                         side='right'), N_EXPERTS - 1)
    h_ext = jnp.concatenate([h, jnp.zeros((1, d), h.dtype)], axis=0)

    def body(acc, blk):
        t, w, e = blk
        y = swiglu(h_ext[t], w_e1[e], w_e3[e], w_e2[e]) * w[:, None]
        return acc.at[t].add(y), None

    acc, _ = lax.scan(body, jnp.zeros_like(h_ext),
                      (tok.reshape(n_blocks, EXPERT_BLOCK), wt.reshape(n_blocks, EXPERT_BLOCK), blk_e))
    return acc[:n]


def moe_ffn(h, w_router, e_bias, w_e1, w_e3, w_e2, w_s1, w_s3, w_s2):
    scores = jax.nn.sigmoid((h @ w_router).astype(F32))
    _, idx = lax.top_k(scores + e_bias.astype(F32), TOP_K)
    gw = jnp.take_along_axis(scores, idx, axis=1)
    gw = gw / jnp.sum(gw, axis=1, keepdims=True) * ROUTED_SCALE
    return swiglu(h, w_s1, w_s3, w_s2) + routed_experts(h, idx, gw.astype(h.dtype), w_e1, w_e3, w_e2)


def setup_inputs(seed: int = 0) -> dict:
    key = jax.random.key(seed)
    ks = iter(jax.random.split(key, 48))
    L, D, G, P, J = DEPTH, D_MODEL, S5_GROUPS, S5_STATE, S5_GROUP_CH
    E, FF = N_EXPERTS, EXPERT_FF

    def nrm(shape, s):
        return jax.random.normal(next(ks), shape, F32) * s

    n_idx = jnp.arange(P, dtype=F32)
    return {
        "x": nrm((BATCH, SEQ, D), 1.0),
        "c": nrm((BATCH, D), 1.0),
        "ctx": nrm((BATCH, CTX_LEN, D), 1.0),
        "c_ctx": nrm((D,), 1.0),
        "w_ada": nrm((L, D, 6 * D), 0.5 * D ** -0.5),
        "b_ada": nrm((L, 6 * D), 0.02),
        "g_norm1": 1.0 + nrm((L, D), 0.02),
        "g_norm2": 1.0 + nrm((L, D), 0.02),
        "w_in": nrm((L, D, IN_W), D ** -0.5),
        "na_rpb": nrm((L, NA_HEADS, 2 * WIN_H - 1, 2 * WIN_W - 1), 0.1),
        "diff_lambda": nrm((L, 4, DIFF_DK), 0.1),
        "g_diff": 1.0 + nrm((L, DIFF_DV), 0.02),
        "g_mla_q": 1.0 + nrm((L, MLA_Q_RANK), 0.02),
        "g_mla_kv": 1.0 + nrm((L, MLA_KV_RANK), 0.02),
        "w_mla_uq": nrm((L, MLA_Q_RANK, MLA_HEADS * (MLA_NOPE + MLA_ROPE)), MLA_Q_RANK ** -0.5),
        "w_mla_ukv": nrm((L, MLA_KV_RANK, MLA_HEADS * (MLA_NOPE + MLA_V)), MLA_KV_RANK ** -0.5),
        "s5_lam_re": -0.5 + nrm((L, 2, G, P), 1e-3),
        "s5_lam_im": math.pi * n_idx + nrm((L, 2, G, P), 1e-3),
        "s5_log_step": jax.random.uniform(next(ks), (L, 2, G), F32, math.log(1e-3), math.log(1e-1)),
        "s5_b_re": nrm((L, 2, G, P, J), (2 * J) ** -0.5),
        "s5_b_im": nrm((L, 2, G, P, J), (2 * J) ** -0.5),
        "s5_c_re": nrm((L, 2, G, J, P), (2 * P) ** -0.5),
        "s5_c_im": nrm((L, 2, G, J, P), (2 * P) ** -0.5),
        "s5_d": nrm((L, G, J), 1.0),
        "w_glu": nrm((L, S5_WIDTH, S5_WIDTH), S5_WIDTH ** -0.5),
        "b_glu": nrm((L, S5_WIDTH), 0.02),
        "w_branch": nrm((L, N_BRANCH, BRANCH_W, D), BRANCH_W ** -0.5),
        "w_out": nrm((L, D, D), D ** -0.5),
        "w_router": nrm((L, D, E), D ** -0.5),
        "e_bias": nrm((L, E), 0.01),
        "w_e1": nrm((L, E, D, FF), D ** -0.5),
        "w_e3": nrm((L, E, D, FF), D ** -0.5),
        "w_e2": nrm((L, E, FF, D), FF ** -0.5),
        "w_s1": nrm((L, D, FF), D ** -0.5),
        "w_s3": nrm((L, D, FF), D ** -0.5),
        "w_s2": nrm((L, FF, D), FF ** -0.5),
        "g_final": 1.0 + nrm((D,), 0.02),
    }


def reference(x, c, ctx, c_ctx, w_ada, b_ada, g_norm1, g_norm2, w_in, na_rpb, diff_lambda, g_diff, g_mla_q,
              g_mla_kv, w_mla_uq, w_mla_ukv, s5_lam_re, s5_lam_im, s5_log_step, s5_b_re, s5_b_im, s5_c_re,
              s5_c_im, s5_d, w_glu, b_glu, w_branch, w_out, w_router, e_bias, w_e1, w_e3, w_e2, w_s1, w_s3,
              w_s2, g_final):
    b, s, d = x.shape
    lc = ctx.shape[1]
    pos = jnp.arange(s, dtype=jnp.int32)
    rows, cols = pos // GRID_W, pos % GRID_W
    xl, xc = x, ctx
    sc, scc = jax.nn.silu(c), jax.nn.silu(c_ctx)
    for l in range(DEPTH):
        ctx_out = l < DEPTH - 1
        sh1, sc1, gt1, sh2, sc2, gt2 = jnp.split((sc @ w_ada[l] + b_ada[l])[:, None, :], 6, axis=-1)
        csh1, csc1, cgt1, csh2, csc2, cgt2 = jnp.split((scc @ w_ada[l] + b_ada[l])[None, None, :], 6, axis=-1)
        hl = modulate(rmsnorm(xl, g_norm1[l]), sh1, sc1)
        hc = modulate(rmsnorm(xc, g_norm1[l]), csh1, csc1)
        yl, yc = hybrid_mixer(hl, hc, rows, cols, l, ctx_out, w_in[l], na_rpb[l], diff_lambda[l], g_diff[l],
                              g_mla_q[l], g_mla_kv[l], w_mla_uq[l], w_mla_ukv[l], s5_lam_re[l], s5_lam_im[l],
                              s5_log_step[l], s5_b_re[l], s5_b_im[l], s5_c_re[l], s5_c_im[l], s5_d[l],
                              w_glu[l], b_glu[l], w_branch[l], w_out[l])
        xl = xl + gt1 * yl
        hl2 = modulate(rmsnorm(xl, g_norm2[l]), sh2, sc2)
        moe_w = (w_router[l], e_bias[l], w_e1[l], w_e3[l], w_e2[l], w_s1[l], w_s3[l], w_s2[l])
        if ctx_out:
            xc = xc + cgt1 * yc
            hc2 = modulate(rmsnorm(xc, g_norm2[l]), csh2, csc2)
            f = moe_ffn(jnp.concatenate([hl2.reshape(-1, d), hc2.reshape(-1, d)], axis=0), *moe_w)
            xl = xl + gt2 * f[:b * s].reshape(b, s, d)
            xc = xc + cgt2 * f[b * s:].reshape(b, lc, d)
        else:
            xl = xl + gt2 * moe_ffn(hl2.reshape(-1, d), *moe_w).reshape(b, s, d)
    return rmsnorm(xl, g_final)
```

```python
import functools
import math

import jax
import jax.numpy as jnp
from jax import lax
from jax.experimental import pallas as pl
from jax.experimental.pallas import tpu as pltpu

F32 = jnp.float32
BF16 = jnp.bfloat16
I32 = jnp.int32
HIGHEST = lax.Precision.HIGHEST

GRID_W = 64
HEAD_DIM = 64
NA_HEADS = 4
WIN_H = 8
WIN_W = 16
DIFF_HEADS = 4
DIFF_DK = 32
DIFF_DV = 64
MLA_HEADS = 4
MLA_NOPE = 64
MLA_ROPE = 32
MLA_V = 64
S5_GROUPS = 16
S5_GROUP_CH = 16
S5_STATE = 64
S5_WIDTH = S5_GROUPS * S5_GROUP_CH
S5_CHUNK = 16
N_BRANCH = 4
N_EXPERTS = 64
TOP_K = 6
EXPERT_BLOCK = 256
ROUTED_SCALE = 1.0
ROPE_BASE = 10000.0
EPS = 1e-6
NEG_INF = -1e30

TM = 256
LANES = 128
HI_MASK = -65536

NA_SCALE = HEAD_DIM ** -0.5
DIFF_SCALE = DIFF_DK ** -0.5
MLA_SCALE = (MLA_NOPE + MLA_ROPE) ** -0.5

OFF_NA, OFF_DQ, OFF_DK, OFF_DV, OFF_CQ, OFF_CKV, OFF_KPE, OFF_U, OFF_GATE = (
    0, 768, 1024, 1280, 1536, 1792, 1920, 2048, 2304)


def _cparams(sem, vmem_mb=None):
    kw = dict(dimension_semantics=sem)
    if vmem_mb is not None:
        kw["vmem_limit_bytes"] = vmem_mb * 2 ** 20
    return pltpu.CompilerParams(**kw)


def _rms(x, g):
    return x * lax.rsqrt(jnp.mean(x * x, axis=-1, keepdims=True) + EPS) * g


def _rope(x, cos, sin_signed):
    w = x.shape[-1]
    lane = lax.broadcasted_iota(I32, x.shape, 1)
    partner = jnp.where((lane & 8) == 0, pltpu.roll(x, w - 8, 1), pltpu.roll(x, 8, 1))
    return x * cos + partner * sin_signed


def _pack_bf16_pair(x):
    n = x.shape[-1] // 2
    bits = lax.bitcast_convert_type(x.astype(BF16).astype(F32), I32)
    return lax.shift_right_logical(bits[:, :n], 16) | (bits[:, n:] & HI_MASK)


def _unpack_bf16_pair(p):
    lo = lax.bitcast_convert_type(lax.shift_left(p, 16), F32)
    hi = lax.bitcast_convert_type(p & HI_MASK, F32)
    return lo, hi


def _ada_kernel(c_ref, w_ref, b_ref, o_ref):
    c = c_ref[...]
    s = c * jax.nn.sigmoid(c)
    o_ref[0] = jnp.dot(s, w_ref[0], preferred_element_type=F32, precision=HIGHEST) + b_ref[0]


def _ada(c8, w_ada, b_ada):
    depth, d, n = w_ada.shape
    tn = 1536
    return pl.pallas_call(
        _ada_kernel,
        grid=(depth, n // tn),
        in_specs=[pl.BlockSpec((8, d), lambda l, j: (0, 0)),
                  pl.BlockSpec((1, d, tn), lambda l, j: (l, 0, j)),
                  pl.BlockSpec((1, 1, tn), lambda l, j: (l, 0, j))],
        out_specs=pl.BlockSpec((1, 8, tn), lambda l, j: (l, 0, j)),
        out_shape=jax.ShapeDtypeStruct((depth, 8, n), F32),
        compiler_params=_cparams(("arbitrary", "arbitrary"), 40),
        name="ada",
    )(c8, w_ada, b_ada.reshape(depth, 1, n))


def _proj_kernel(n_lat, x_ref, g_ref, sh_ref, sc_ref, w_ref, cos_ref, sin_ref, gq_ref, gkv_ref, wuq_ref, wukv_ref,
                 naq, nak, nav, dfq, dfqr, dfk, dfkr, dfv, mlqn, mlqp, mlqpr, mlkn, mlv, mlkx, s5u, gates):
    is_ctx = pl.program_id(1) >= n_lat
    m_lat = jnp.where(is_ctx, 0.0, 1.0).astype(F32)
    m_ctx = 1.0 - m_lat
    h = (_rms(x_ref[0], g_ref[...]) * (1.0 + sc_ref[0]) + sh_ref[0]).astype(BF16)

    def proj(off, width):
        return jnp.dot(h, w_ref[:, off:off + width], preferred_element_type=F32)

    cos = cos_ref[...]
    sin = sin_ref[...]
    a = proj(OFF_NA, 768)
    naq[0] = (a[:, :256] * NA_SCALE).astype(BF16)
    nak[0] = a[:, 256:512].astype(BF16)
    nav[0] = a[:, 512:].astype(BF16)

    bq = proj(OFF_DQ, 256) * DIFF_SCALE
    dfq[0] = bq.astype(BF16)
    dfqr[0] = _rope(bq, cos, sin).astype(BF16)
    bk = proj(OFF_DK, 256)
    dfk[0] = (bk * m_ctx).astype(BF16)
    dfkr[0] = (_rope(bk, cos, sin) * m_lat).astype(BF16)
    dfv[0] = proj(OFF_DV, 256).astype(BF16)

    cq = _rms(proj(OFF_CQ, 256), gq_ref[...]).astype(BF16)
    q2 = jnp.dot(cq, wuq_ref[...], preferred_element_type=F32) * MLA_SCALE
    qp = q2[:, 256:]
    mlqn[0] = q2[:, :256].astype(BF16)
    mlqp[0] = qp.astype(BF16)
    mlqpr[0] = _rope(qp, cos[:, :LANES], sin[:, :LANES]).astype(BF16)
    ckv = _rms(proj(OFF_CKV, 128), gkv_ref[...]).astype(BF16)
    kv = jnp.dot(ckv, wukv_ref[...], preferred_element_type=F32)
    mlkn[0] = kv[:, :256].astype(BF16)
    mlv[0] = kv[:, 256:].astype(BF16)
    kpe = proj(OFF_KPE, LANES)
    kper = _rope(kpe, cos[:, :LANES], sin[:, :LANES])
    mlkx[0] = (kper * m_lat + pltpu.roll(kpe, MLA_ROPE, 1) * m_ctx).astype(BF16)

    s5u[0] = proj(OFF_U, 256)
    d = x_ref.shape[-1]
    for i in range(N_BRANCH):
        gates[0, :, i * d:(i + 1) * d] = jax.nn.sigmoid(proj(OFF_GATE + i * d, d)).astype(BF16)


def _proj(x_all, g1, shift, scale, w_main, cos_t, sin_t, g_q, g_kv, w_uq, w_ukv, n_lat):
    b, t, d = x_all.shape
    nt = t // TM
    mod_spec = pl.BlockSpec((1, 1, d), lambda bi, i: (bi * 2 + (i >= n_lat).astype(I32), 0, 0))

    def full(a):
        return pl.BlockSpec(a.shape, lambda bi, i: (0,) * a.ndim)

    def tile(w):
        return pl.BlockSpec((1, TM, w), lambda bi, i: (bi, i, 0))

    widths = [256, 256, 256, 256, 256, 256, 256, 256, 256, 128, 128, 256, 256, 128, 256, N_BRANCH * d]
    dtypes = [BF16] * 14 + [F32, BF16]
    return pl.pallas_call(
        functools.partial(_proj_kernel, n_lat),
        grid=(b, nt),
        in_specs=[tile(d), full(g1), mod_spec, mod_spec, full(w_main),
                  pl.BlockSpec((TM, 256), lambda bi, i: (i, 0)), pl.BlockSpec((TM, 256), lambda bi, i: (i, 0)),
                  full(g_q), full(g_kv), full(w_uq), full(w_ukv)],
        out_specs=[tile(w) for w in widths],
        out_shape=[jax.ShapeDtypeStruct((b, t, w), dt) for w, dt in zip(widths, dtypes)],
        compiler_params=_cparams(("arbitrary", "arbitrary"), 56),
        name="proj",
    )(x_all, g1, shift, scale, w_main, cos_t, sin_t, g_q, g_kv, w_uq, w_ukv)


def _na_kernel(q_ref, k0, k1, k2, kc, v0, v1, v2, vc, bias_ref, o_ref):
    nt_dims = (((1,), (1,)), ((), ()))
    outs = []
    for h in range(NA_HEADS):
        sl = slice(h * HEAD_DIM, (h + 1) * HEAD_DIM)
        qh = q_ref[0, :, sl]
        sw = jnp.concatenate(
            [lax.dot_general(qh, kr[0, :, sl], nt_dims, preferred_element_type=F32) for kr in (k0, k1, k2)],
            axis=1) + bias_ref[0, h]
        sc = lax.dot_general(qh, kc[0, :, sl], nt_dims, preferred_element_type=F32)
        m = jnp.maximum(jnp.max(sw, axis=1, keepdims=True), jnp.max(sc, axis=1, keepdims=True))
        pw = jnp.exp(sw - m)
        pc = jnp.exp(sc - m)
        l = jnp.sum(pw, axis=1, keepdims=True) + jnp.sum(pc, axis=1, keepdims=True)
        o = jnp.dot(pc.astype(BF16), vc[0, :, sl], preferred_element_type=F32)
        for j, vr in enumerate((v0, v1, v2)):
            o = o + jnp.dot(pw[:, j * TM:(j + 1) * TM].astype(BF16), vr[0, :, sl], preferred_element_type=F32)
        outs.append(o / l)
    o_ref[0] = jnp.concatenate(outs, axis=1).astype(BF16)


def _na_bias(rpb):
    rows_q = TM // GRID_W
    rows_k = 3 * rows_q
    qr = jnp.arange(rows_q)
    kr = jnp.arange(rows_k)
    col = jnp.arange(GRID_W)
    cstart = jnp.clip(col - WIN_W // 2, 0, GRID_W - WIN_W)
    col_ok = (col[None, :] >= cstart[:, None]) & (col[None, :] < cstart[:, None] + WIN_W)
    dc = jnp.clip(col[None, :] - col[:, None], 1 - WIN_W, WIN_W - 1) + (WIN_W - 1)
    kinds = []
    for start_rel, q_rel in ((jnp.zeros_like(qr), qr), (qr, qr + 4), (jnp.full_like(qr, 4), qr + 8)):
        row_ok = (kr[None, :] >= start_rel[:, None]) & (kr[None, :] < start_rel[:, None] + WIN_H)
        dr = jnp.clip(kr[None, :] - q_rel[:, None] + (WIN_H - 1), 0, 2 * WIN_H - 2)
        bias = rpb.astype(F32)[:, dr[:, None, :, None], dc[None, :, None, :]]
        ok = row_ok[:, None, :, None] & col_ok[None, :, None, :]
        kinds.append(jnp.where(ok[None], bias, NEG_INF).reshape(NA_HEADS, TM, 3 * TM))
    kinds.append(jnp.full((NA_HEADS, TM, 3 * TM), NEG_INF, F32))
    return jnp.stack(kinds)


def _na(q, k, v, bias, n_lat):
    b, t, w = q.shape
    nt = t // TM

    def base(i):
        return jnp.clip(i - 1, 0, n_lat - 3)

    def kind(i):
        return jnp.where(i >= n_lat, 3, jnp.where(i == 0, 0, jnp.where(i == n_lat - 1, 2, 1)))

    def win(j):
        return pl.BlockSpec((1, TM, w), lambda bi, i: (bi, base(i) + j, 0))

    ctx = pl.BlockSpec((1, TM, w), lambda bi, i: (bi, n_lat, 0))
    own = pl.BlockSpec((1, TM, w), lambda bi, i: (bi, i, 0))
    return pl.pallas_call(
        _na_kernel,
        grid=(b, nt),
        in_specs=[own, win(0), win(1), win(2), ctx, win(0), win(1), win(2), ctx,
                  pl.BlockSpec((1, NA_HEADS, TM, 3 * TM), lambda bi, i: (kind(i), 0, 0, 0))],
        out_specs=own,
        out_shape=jax.ShapeDtypeStruct((b, t, w), BF16),
        compiler_params=_cparams(("arbitrary", "arbitrary"), 40),
        name="na_attn",
    )(q, k, k, k, k, v, v, v, v, bias)


def _softmax_step(q, kt, v, carry):
    m, l, acc = carry
    s = jnp.dot(q, kt, preferred_element_type=F32)
    m_new = jnp.maximum(m, jnp.max(s, axis=1, keepdims=True))
    alpha = jnp.exp(m - m_new)
    p = jnp.exp(s - m_new)
    l = alpha * l + jnp.sum(p, axis=1, keepdims=True)
    acc = alpha * acc + jnp.dot(p.astype(BF16), v, preferred_element_type=F32)
    return m_new, l, acc


def _attend(n_maps, s_lat, tk, q_ref, kt_ref, v_ref):
    tq = q_ref.shape[2]
    t = v_ref.shape[2]
    dv = v_ref.shape[3]
    dk = q_ref.shape[3] // n_maps
    is_ctx = pl.program_id(2) * tq >= s_lat
    n_steps = jnp.where(is_ctx, 0, s_lat // tk)
    qs = [q_ref[0, 0, :, m * dk:(m + 1) * dk] for m in range(n_maps)]

    def body(j, carry):
        off = pl.multiple_of(j * tk, tk)
        v = v_ref[0, 0, pl.ds(off, tk), :]
        return tuple(_softmax_step(qs[m], kt_ref[0, 0, m * dk:(m + 1) * dk, pl.ds(off, tk)], v, carry[m])
                     for m in range(n_maps))

    init = tuple((jnp.full((tq, 1), NEG_INF, F32), jnp.zeros((tq, 1), F32), jnp.zeros((tq, dv), F32))
                 for _ in range(n_maps))
    carry = lax.fori_loop(0, n_steps, body, init)
    vc = v_ref[0, 0, s_lat:t, :]
    outs = []
    for m in range(n_maps):
        _, l, acc = _softmax_step(qs[m], kt_ref[0, 0, m * dk:(m + 1) * dk, s_lat:t], vc, carry[m])
        outs.append(acc / l)
    return outs


def _mla_kernel(s_lat, tk, q_ref, kt_ref, v_ref, o_ref):
    (o,) = _attend(1, s_lat, tk, q_ref, kt_ref, v_ref)
    o_ref[0, 0] = o.astype(BF16)


def _diff_kernel(s_lat, tk, q_ref, kt_ref, v_ref, lam_ref, g_ref, o_ref, *, lam_init):
    o1, o2 = _attend(2, s_lat, tk, q_ref, kt_ref, v_ref)
    lam = lam_ref[...]
    lam_full = (jnp.exp(jnp.sum(lam[0:1] * lam[1:2], axis=1, keepdims=True))
                - jnp.exp(jnp.sum(lam[2:3] * lam[3:4], axis=1, keepdims=True)) + lam_init)
    d = _rms(o1 - lam_full * o2, g_ref[...]) * (1.0 - lam_init)
    o_ref[0, 0] = d.astype(BF16)


def _global_attention(kernel, name, q, kt, v, extra, s_lat):
    b, h, t, dq = q.shape
    dv = v.shape[-1]
    tk = min(1024, s_lat)
    in_specs = [pl.BlockSpec((1, 1, TM, dq), lambda bi, hi, i: (bi, hi, i, 0)),
                pl.BlockSpec((1, 1, kt.shape[2], t), lambda bi, hi, i: (bi, hi, 0, 0)),
                pl.BlockSpec((1, 1, t, dv), lambda bi, hi, i: (bi, hi, 0, 0))]
    in_specs += [pl.BlockSpec(a.shape, lambda bi, hi, i: (0, 0)) for a in extra]
    return pl.pallas_call(
        functools.partial(kernel, s_lat, tk),
        grid=(b, h, t // TM),
        in_specs=in_specs,
        out_specs=pl.BlockSpec((1, 1, TM, dv), lambda bi, hi, i: (bi, hi, i, 0)),
        out_shape=jax.ShapeDtypeStruct((b, h, t, dv), BF16),
        compiler_params=_cparams(("arbitrary", "arbitrary", "arbitrary"), 48),
        name=name,
    )(q, kt, v, *extra)


def _s5_tables(lam_re, lam_im, log_step, b_re, b_im, c_re, c_im):
    c = S5_CHUNK
    lr, li = lam_re.astype(F32), lam_im.astype(F32)
    step = jnp.exp(log_step.astype(F32))[..., None]
    mag = jnp.exp(lr * step)
    ar, ai = mag * jnp.cos(li * step), mag * jnp.sin(li * step)
    den = lr * lr + li * li
    fr = ((ar - 1.0) * lr + ai * li) / den
    fi = (ai * lr - (ar - 1.0) * li) / den
    br, bi = b_re.astype(F32), b_im.astype(F32)
    bbr = fr[..., None] * br - fi[..., None] * bi
    bbi = fr[..., None] * bi + fi[..., None] * br
    tau = jnp.arange(c + 1, dtype=F32)[:, None, None, None]
    pmag = jnp.exp(lr * step * tau)
    pr, pi = pmag * jnp.cos(li * step * tau), pmag * jnp.sin(li * step * tau)
    cr, ci = c_re.astype(F32), c_im.astype(F32)
    abr = pr[..., None] * bbr - pi[..., None] * bbi
    abi = pr[..., None] * bbi + pi[..., None] * bbr
    kk = jnp.einsum('dgip,tdgpj->tdgij', cr, abr) - jnp.einsum('dgip,tdgpj->tdgij', ci, abi)
    s_idx = jnp.arange(c)[:, None]
    t_idx = jnp.arange(c)[None, :]
    lag_f = jnp.clip(t_idx - s_idx, 0, c)
    lag_b = jnp.clip(s_idx - t_idx, 0, c)
    tf = jnp.where((s_idx <= t_idx)[:, :, None, None, None], kk[lag_f, 0], 0.0)
    tb = jnp.where((s_idx >= t_idx)[:, :, None, None, None], kk[lag_b, 1], 0.0)
    g, j_ch = S5_GROUPS, S5_GROUP_CH
    tmat = (tf + tb).transpose(2, 0, 4, 1, 3).reshape(g, c * j_ch, c * j_ch)
    pow_f = jnp.arange(c - 1, -1, -1)
    pow_b = jnp.arange(c)

    def exit_map(d, pw):
        re = abr[pw, d].transpose(1, 0, 3, 2).reshape(g, c * j_ch, S5_STATE)
        im = abi[pw, d].transpose(1, 0, 3, 2).reshape(g, c * j_ch, S5_STATE)
        return jnp.concatenate([re, im], axis=-1)

    bsum = jnp.stack([exit_map(0, pow_f), exit_map(1, pow_b)])

    def entry_map(d, pw):
        p_r, p_i = pr[pw, d], pi[pw, d]
        on_re = jnp.einsum('gip,tgp->gpti', cr[d], p_r) - jnp.einsum('gip,tgp->gpti', ci[d], p_i)
        on_im = -jnp.einsum('gip,tgp->gpti', cr[d], p_i) - jnp.einsum('gip,tgp->gpti', ci[d], p_r)
        return jnp.concatenate([on_re, on_im], axis=1).reshape(g, 2 * S5_STATE, c * j_ch)

    cin = jnp.stack([entry_map(0, jnp.arange(1, c + 1)), entry_map(1, jnp.arange(c, 0, -1))])
    a_c = (pr[c], pi[c])
    return tmat.astype(BF16), bsum.astype(BF16), cin.astype(BF16), a_c


def _s5_intra_kernel(u_ref, tm_ref, bs_ref, y_ref, sr_ref, si_ref):
    u = u_ref[0, 0]
    y_ref[0, 0] = jnp.dot(u, tm_ref[0], preferred_element_type=F32)
    for d in range(2):
        s = jnp.dot(u, bs_ref[d, 0], preferred_element_type=F32)
        sr_ref[0, d, 0] = s[:, :S5_STATE]
        si_ref[0, d, 0] = s[:, S5_STATE:]


def _s5_intra(ug, tmat, bsum):
    b, g, nc, w = ug.shape
    return pl.pallas_call(
        _s5_intra_kernel,
        grid=(b, g),
        in_specs=[pl.BlockSpec((1, 1, nc, w), lambda bi, gi: (bi, gi, 0, 0)),
                  pl.BlockSpec((1, w, w), lambda bi, gi: (gi, 0, 0)),
                  pl.BlockSpec((2, 1, w, 2 * S5_STATE), lambda bi, gi: (0, gi, 0, 0))],
        out_specs=[pl.BlockSpec((1, 1, nc, w), lambda bi, gi: (bi, gi, 0, 0)),
                   pl.BlockSpec((1, 2, 1, nc, S5_STATE), lambda bi, gi: (bi, 0, gi, 0, 0)),
                   pl.BlockSpec((1, 2, 1, nc, S5_STATE), lambda bi, gi: (bi, 0, gi, 0, 0))],
        out_shape=[jax.ShapeDtypeStruct((b, g, nc, w), F32),
                   jax.ShapeDtypeStruct((b, 2, g, nc, S5_STATE), F32),
                   jax.ShapeDtypeStruct((b, 2, g, nc, S5_STATE), F32)],
        compiler_params=_cparams(("arbitrary", "arbitrary")),
        name="s5_intra",
    )(ug, tmat, bsum)


def _s5_scan_kernel(sr_ref, si_ref, ar_ref, ai_ref, hr_ref, hi_ref, st_r, st_i):
    @pl.when(pl.program_id(0) == 0)
    def _():
        st_r[...] = jnp.zeros_like(st_r)
        st_i[...] = jnp.zeros_like(st_i)

    ar = ar_ref[...]
    ai = ai_ref[...]

    def body(k, carry):
        hr, hi = carry
        hr_ref[k] = hr
        hi_ref[k] = hi
        return ar * hr - ai * hi + sr_ref[k], ar * hi + ai * hr + si_ref[k]

    hr, hi = lax.fori_loop(0, sr_ref.shape[0], body, (st_r[...], st_i[...]))
    st_r[...] = hr
    st_i[...] = hi


def _s5_scan(sr, si, ar, ai):
    n, r, p = sr.shape
    kb = next(c for c in (104, 80, 64, 40, 16, 8, 4, 2, 1) if n % c == 0)
    blk = pl.BlockSpec((kb, r, p), lambda i: (i, 0, 0))
    par = pl.BlockSpec((r, p), lambda i: (0, 0))
    return pl.pallas_call(
        _s5_scan_kernel,
        grid=(n // kb,),
        in_specs=[blk, blk, par, par],
        out_specs=[blk, blk],
        out_shape=[jax.ShapeDtypeStruct((n, r, p), F32)] * 2,
        scratch_shapes=[pltpu.VMEM((r, p), F32), pltpu.VMEM((r, p), F32)],
        compiler_params=_cparams(("arbitrary",), 40),
        name="s5_scan",
    )(sr, si, ar, ai)


def _s5_inter_kernel(y_ref, h_ref, cin_ref, o_ref):
    y = y_ref[0, 0]
    for d in range(2):
        y = y + jnp.dot(h_ref[0, d, 0], cin_ref[d, 0], preferred_element_type=F32)
    o_ref[0, 0] = y


def _s5_inter(y_intra, hin, cin):
    b, g, nc, w = y_intra.shape
    return pl.pallas_call(
        _s5_inter_kernel,
        grid=(b, g),
        in_specs=[pl.BlockSpec((1, 1, nc, w), lambda bi, gi: (bi, gi, 0, 0)),
                  pl.BlockSpec((1, 2, 1, nc, 2 * S5_STATE), lambda bi, gi: (bi, 0, gi, 0, 0)),
                  pl.BlockSpec((2, 1, 2 * S5_STATE, w), lambda bi, gi: (0, gi, 0, 0))],
        out_specs=pl.BlockSpec((1, 1, nc, w), lambda bi, gi: (bi, gi, 0, 0)),
        out_shape=jax.ShapeDtypeStruct((b, g, nc, w), F32),
        compiler_params=_cparams(("arbitrary", "arbitrary")),
        name="s5_inter",
    )(y_intra, hin, cin)


def _s5_mixer(u, tables, s_lat):
    tmat, bsum, cin, (acr, aci) = tables
    b, t, w = u.shape
    c, g, jc = S5_CHUNK, S5_GROUPS, S5_GROUP_CH
    nc, ncl = t // c, s_lat // c
    ug = u.reshape(b, nc, c, g, jc).transpose(0, 3, 1, 2, 4).reshape(b, g, nc, c * jc).astype(BF16)
    y_intra, sr, si = _s5_intra(ug, tmat, bsum)

    def to_seq(s):
        fwd = jnp.concatenate([s[:, 0, :, ncl:], s[:, 0, :, :ncl]], axis=2)
        bwd = jnp.concatenate([s[:, 1, :, ncl:][:, :, ::-1], s[:, 1, :, :ncl][:, :, ::-1]], axis=2)
        return jnp.stack([fwd, bwd]).transpose(3, 0, 1, 2, 4).reshape(nc, 2 * b * g, S5_STATE)

    def from_seq(h):
        h = h.reshape(nc, 2, b, g, S5_STATE).transpose(1, 2, 3, 0, 4)
        ncc = nc - ncl
        fwd = jnp.concatenate([h[0][:, :, ncc:], h[0][:, :, :ncc]], axis=2)
        bwd = jnp.concatenate([h[1][:, :, ncc:][:, :, ::-1], h[1][:, :, :ncc][:, :, ::-1]], axis=2)
        return jnp.stack([fwd, bwd], axis=1)

    def rows(a):
        return jnp.broadcast_to(a[:, None], (2, b, g, S5_STATE)).reshape(2 * b * g, S5_STATE)

    hr, hi = _s5_scan(to_seq(sr), to_seq(si), rows(acr), rows(aci))
    hin = jnp.concatenate([from_seq(hr), from_seq(hi)], axis=-1).astype(BF16)
    y = _s5_inter(y_intra, hin, cin)
    return y.reshape(b, g, nc, c, jc).transpose(0, 2, 3, 1, 4).reshape(b, t, w)


def _merge_kernel(ya, yb, ym, ys, su, dd, wglu, bglu, gates, wb, wo, x_ref, gt1, g2, sh2, sc2, wr, eb,
                  xo_ref, hp_ref, idx_ref, gw_ref):
    d = x_ref.shape[-1]
    y5 = ys[0] + dd[...] * su[0]
    gl = jax.nn.gelu(y5)
    yd = gl * jax.nn.sigmoid(jnp.dot(gl, wglu[...], preferred_element_type=F32) + bglu[...])
    branches = (ya[0], yb[0], ym[0], yd.astype(BF16))
    m = None
    for i in range(N_BRANCH):
        term = gates[0, :, i * d:(i + 1) * d].astype(F32) * jnp.dot(branches[i], wb[i], preferred_element_type=F32)
        m = term if m is None else m + term
    xn = x_ref[0] + gt1[0] * jnp.dot(m.astype(BF16), wo[...], preferred_element_type=F32)
    xo_ref[0] = xn
    h2 = _rms(xn, g2[...]) * (1.0 + sc2[0]) + sh2[0]
    hp_ref[0] = _pack_bf16_pair(h2)

    scores = jax.nn.sigmoid(jnp.dot(h2, wr[...], preferred_element_type=F32, precision=HIGHEST))
    sel = scores + eb[...]
    lane = lax.broadcasted_iota(I32, sel.shape, 1).astype(F32)
    idx_acc = jnp.zeros(sel.shape, F32)
    gw_acc = jnp.zeros(sel.shape, F32)
    for k in range(TOP_K):
        mx = jnp.max(sel, axis=1, keepdims=True)
        ik = jnp.min(jnp.where(sel == mx, lane, float(LANES)), axis=1, keepdims=True)
        hit = lane == ik
        gk = jnp.sum(jnp.where(hit, scores, 0.0), axis=1, keepdims=True)
        idx_acc = jnp.where(lane == k, ik, idx_acc)
        gw_acc = jnp.where(lane == k, gk, gw_acc)
        sel = jnp.where(hit, -jnp.inf, sel)
    idx_ref[0] = idx_acc.astype(I32)
    gw_ref[0] = gw_acc / jnp.sum(gw_acc, axis=1, keepdims=True) * ROUTED_SCALE


def _merge(ya, yb, ym, ys, su, dd, wglu, bglu, gates, wb, wo, x_all, gt1, g2, sh2, sc2, wr, eb, n_lat):
    b, t, d = x_all.shape
    mod_spec = pl.BlockSpec((1, 1, d), lambda bi, i: (bi * 2 + (i >= n_lat).astype(I32), 0, 0))

    def full(a):
        return pl.BlockSpec(a.shape, lambda bi, i: (0,) * a.ndim)

    def tile(w):
        return pl.BlockSpec((1, TM, w), lambda bi, i: (bi, i, 0))

    return pl.pallas_call(
        _merge_kernel,
        grid=(b, t // TM),
        in_specs=[tile(256), tile(256), tile(256), tile(256), tile(256), full(dd), full(wglu), full(bglu),
                  tile(N_BRANCH * d), full(wb), full(wo), tile(d), mod_spec, full(g2), mod_spec, mod_spec,
                  full(wr), full(eb)],
        out_specs=[tile(d), tile(d // 2), tile(LANES), tile(LANES)],
        out_shape=[jax.ShapeDtypeStruct((b, t, d), F32), jax.ShapeDtypeStruct((b, t, d // 2), I32),
                   jax.ShapeDtypeStruct((b, t, LANES), I32), jax.ShapeDtypeStruct((b, t, LANES), F32)],
        compiler_params=_cparams(("arbitrary", "arbitrary"), 48),
        name="merge",
    )(ya, yb, ym, ys, su, dd, wglu, bglu, gates, wb, wo, x_all, gt1, g2, sh2, sc2, wr, eb)


def _dispatch(idx):
    n = idx.shape[0]
    n_assign = n * TOP_K
    e_flat = idx.reshape(-1)
    order = jnp.argsort(e_flat).astype(I32)
    e_s = e_flat[order]
    t_s = order // TOP_K
    counts = jnp.bincount(e_flat, length=N_EXPERTS).astype(I32)
    starts = jnp.cumsum(counts) - counts
    padded = (counts + EXPERT_BLOCK - 1) // EXPERT_BLOCK * EXPERT_BLOCK
    pends = jnp.cumsum(padded)
    dest = (pends - padded)[e_s] + jnp.arange(n_assign, dtype=I32) - starts[e_s]
    n_blocks = -(-n_assign // EXPERT_BLOCK) + N_EXPERTS
    tok = jnp.zeros((n_blocks * EXPERT_BLOCK,), I32).at[dest].set(t_s)
    blk_e = jnp.minimum(jnp.searchsorted(pends, jnp.arange(n_blocks, dtype=I32) * EXPERT_BLOCK, side='right'),
                        N_EXPERTS - 1).astype(I32)
    pos = jnp.zeros((n_assign,), I32).at[order].set(dest).reshape(n, TOP_K)
    return tok, blk_e, pos


def _gather_kernel(tok_ref, src_ref, dst_ref, sem):
    base = pl.program_id(0) * EXPERT_BLOCK

    def row_copy(r, src_row):
        return pltpu.make_async_copy(src_ref.at[pl.ds(src_row, 1)], dst_ref.at[pl.ds(base + r, 1)], sem)

    def issue(r, c):
        row_copy(r, tok_ref[0, 0, r]).start()
        return c

    def drain(r, c):
        row_copy(r, 0).wait()
        return c

    lax.fori_loop(0, EXPERT_BLOCK, issue, 0)
    lax.fori_loop(0, EXPERT_BLOCK, drain, 0)


def _gather_rows(tok, src):
    n_blocks = tok.shape[0] // EXPERT_BLOCK
    return pl.pallas_call(
        _gather_kernel,
        grid=(n_blocks,),
        in_specs=[pl.BlockSpec((1, 1, EXPERT_BLOCK), lambda i: (i, 0, 0), memory_space=pltpu.SMEM),
                  pl.BlockSpec(memory_space=pl.ANY)],
        out_specs=pl.BlockSpec(memory_space=pl.ANY),
        out_shape=jax.ShapeDtypeStruct((tok.shape[0], src.shape[1]), src.dtype),
        scratch_shapes=[pltpu.SemaphoreType.DMA(())],
        compiler_params=_cparams(("arbitrary",)),
        name="moe_gather",
    )(tok.reshape(n_blocks, 1, EXPERT_BLOCK), src)


def _swiglu_packed(p, w1, w3, w2):
    lo, hi = _unpack_bf16_pair(p)
    lo, hi = lo.astype(BF16), hi.astype(BF16)
    n = lo.shape[-1]

    def up(w):
        return (jnp.dot(lo, w[:n], preferred_element_type=F32) + jnp.dot(hi, w[n:], preferred_element_type=F32))

    a = up(w1)
    return jnp.dot((a * jax.nn.sigmoid(a) * up(w3)).astype(BF16), w2, preferred_element_type=F32)


def _ffn_kernel(be_ref, x_ref, w1_ref, w3_ref, w2_ref, y_ref):
    del be_ref
    y_ref[...] = _pack_bf16_pair(_swiglu_packed(x_ref[...], w1_ref[0], w3_ref[0], w2_ref[0]))


def _expert_ffn(blk_e, xs, w1, w3, w2):
    n_slots, half = xs.shape
    e, d, ff = w1.shape
    row = pl.BlockSpec((EXPERT_BLOCK, half), lambda i, be: (i, 0))
    return pl.pallas_call(
        _ffn_kernel,
        grid_spec=pltpu.PrefetchScalarGridSpec(
            num_scalar_prefetch=1,
            grid=(n_slots // EXPERT_BLOCK,),
            in_specs=[row,
                      pl.BlockSpec((1, d, ff), lambda i, be: (be[i], 0, 0)),
                      pl.BlockSpec((1, d, ff), lambda i, be: (be[i], 0, 0)),
                      pl.BlockSpec((1, ff, d), lambda i, be: (be[i], 0, 0))],
            out_specs=row),
        out_shape=jax.ShapeDtypeStruct((n_slots, half), I32),
        compiler_params=_cparams(("arbitrary",)),
        name="moe_ffn",
    )(blk_e, xs, w1, w3, w2)


def _combine_kernel(pos_ref, gw_ref, ys_ref, hp_ref, x_ref, gt2, ws1, ws3, ws2, xo_ref, buf, sem):
    def row_copy(k, r, src_row):
        return pltpu.make_async_copy(ys_ref.at[pl.ds(src_row, 1)], buf.at[k, pl.ds(r, 1)], sem)

    for k in range(TOP_K):
        def issue(r, c, k=k):
            row_copy(k, r, pos_ref[0, 0, k * TM + r]).start()
            return c
        lax.fori_loop(0, TM, issue, 0)

    shared = _swiglu_packed(hp_ref[...], ws1[...], ws3[...], ws2[...])

    for k in range(TOP_K):
        def drain(r, c, k=k):
            row_copy(k, r, 0).wait()
            return c
        lax.fori_loop(0, TM, drain, 0)

    gw = gw_ref[...]
    lo = hi = None
    for k in range(TOP_K):
        wk = gw[:, k:k + 1]
        lk, hk = _unpack_bf16_pair(buf[k])
        lo = wk * lk if lo is None else lo + wk * lk
        hi = wk * hk if hi is None else hi + wk * hk
    xo_ref[...] = x_ref[...] + gt2[0] * (shared + jnp.concatenate([lo, hi], axis=1))


def _combine(pos, gw, ys, hp, x_flat, gt2, ws1, ws3, ws2, tiles_per_batch, n_lat):
    n, d = x_flat.shape
    nt = n // TM
    pos3 = pos.reshape(nt, TM, TOP_K).transpose(0, 2, 1).reshape(nt, 1, TOP_K * TM)

    def full(a):
        return pl.BlockSpec(a.shape, lambda i: (0,) * a.ndim)

    def tile(w):
        return pl.BlockSpec((TM, w), lambda i: (i, 0))

    def mod_idx(i):
        return (i // tiles_per_batch) * 2 + ((i % tiles_per_batch) >= n_lat).astype(I32)

    return pl.pallas_call(
        _combine_kernel,
        grid=(nt,),
        in_specs=[pl.BlockSpec((1, 1, TOP_K * TM), lambda i: (i, 0, 0), memory_space=pltpu.SMEM),
                  tile(LANES), pl.BlockSpec(memory_space=pl.ANY), tile(d // 2), tile(d),
                  pl.BlockSpec((1, 1, d), lambda i: (mod_idx(i), 0, 0)), full(ws1), full(ws3), full(ws2)],
        out_specs=tile(d),
        out_shape=jax.ShapeDtypeStruct((n, d), F32),
        scratch_shapes=[pltpu.VMEM((TOP_K, TM, d // 2), I32), pltpu.SemaphoreType.DMA(())],
        compiler_params=_cparams(("arbitrary",), 40),
        name="moe_combine",
    )(pos3, gw, ys, hp, x_flat, gt2, ws1, ws3, ws2)


def _final_kernel(x_ref, g_ref, o_ref):
    o_ref[0] = _rms(x_ref[0], g_ref[...])


def _final_norm(x_all, g, s_lat):
    b, _, d = x_all.shape
    return pl.pallas_call(
        _final_kernel,
        grid=(b, s_lat // TM),
        in_specs=[pl.BlockSpec((1, TM, d), lambda bi, i: (bi, i, 0)), pl.BlockSpec((1, d), lambda bi, i: (0, 0))],
        out_specs=pl.BlockSpec((1, TM, d), lambda bi, i: (bi, i, 0)),
        out_shape=jax.ShapeDtypeStruct((b, s_lat, d), F32),
        compiler_params=_cparams(("arbitrary", "arbitrary")),
        name="final_norm",
    )(x_all, g.reshape(1, d))


def _rope_tables(s_lat, t):
    pos = jnp.arange(s_lat, dtype=I32)
    half = 8
    freqs = ROPE_BASE ** (-jnp.arange(half, dtype=F32) * 2.0 / (2 * half))

    def part(p):
        ang = p.astype(F32)[:, None] * freqs[None, :]
        c, s = jnp.cos(ang), jnp.sin(ang)
        return jnp.concatenate([c, c], axis=1), jnp.concatenate([-s, s], axis=1)

    cr, sr = part(pos // GRID_W)
    cc, sc = part(pos % GRID_W)
    cos = jnp.tile(jnp.concatenate([cr, cc], axis=1), (1, 8))
    sin = jnp.tile(jnp.concatenate([sr, sc], axis=1), (1, 8))
    pad = t - s_lat
    return (jnp.concatenate([cos, jnp.ones((pad, 256), F32)], axis=0),
            jnp.concatenate([sin, jnp.zeros((pad, 256), F32)], axis=0))


def _heads(x, h):
    b, t, w = x.shape
    return x.reshape(b, t, h, w // h).transpose(0, 2, 1, 3)


def _tokens(o):
    b, h, t, d = o.shape
    return o.transpose(0, 2, 1, 3).reshape(b, t, h * d)


def kernel(x, c, ctx, c_ctx, w_ada, b_ada, g_norm1, g_norm2, w_in, na_rpb, diff_lambda, g_diff, g_mla_q, g_mla_kv, w_mla_uq, w_mla_ukv, s5_lam_re, s5_lam_im, s5_log_step, s5_b_re, s5_b_im, s5_c_re, s5_c_im, s5_d, w_glu, b_glu, w_branch, w_out, w_router, e_bias, w_e1, w_e3, w_e2, w_s1, w_s3, w_s2, g_final):
    b, s_lat, d = x.shape
    lc = ctx.shape[1]
    t = s_lat + lc
    depth = w_ada.shape[0]
    n_lat = s_lat // TM
    assert lc == TM and s_lat % TM == 0 and n_lat >= 3 and b + 1 <= 8 and d % 256 == 0
    tiles_per_batch = t // TM

    x_all = jnp.concatenate([x, ctx], axis=1)
    c8 = jnp.zeros((8, d), F32).at[:b].set(c).at[b].set(c_ctx)
    ada = _ada(c8, w_ada, b_ada)
    cos_t, sin_t = _rope_tables(s_lat, t)

    hq = jnp.arange(MLA_HEADS)[:, None] * (MLA_NOPE + MLA_ROPE)
    uq_cols = jnp.concatenate([(hq + jnp.arange(MLA_NOPE)[None]).reshape(-1),
                               (hq + MLA_NOPE + jnp.arange(MLA_ROPE)[None]).reshape(-1)])
    hk = jnp.arange(MLA_HEADS)[:, None] * (MLA_NOPE + MLA_V)
    ukv_cols = jnp.concatenate([(hk + jnp.arange(MLA_NOPE)[None]).reshape(-1),
                                (hk + MLA_NOPE + jnp.arange(MLA_V)[None]).reshape(-1)])
    kpe_end = 1952

    for l in range(depth):
        mods = jnp.concatenate([ada[l, :b, None, :], jnp.broadcast_to(ada[l, b], (b, 1, 6 * d))], axis=1)
        sh1, sc1, gt1, sh2, sc2, gt2 = (mods[:, :, i * d:(i + 1) * d].reshape(2 * b, 1, d) for i in range(6))
        w_main = jnp.concatenate([w_in[l][:, :kpe_end], jnp.zeros((d, OFF_U - kpe_end), F32),
                                  w_in[l][:, kpe_end:]], axis=1).astype(BF16)
        (naq, nak, nav, dfq, dfqr, dfk, dfkr, dfv, mlqn, mlqp, mlqpr, mlkn, mlv, mlkx, s5u, gates) = _proj(
            x_all, g_norm1[l].reshape(1, d), sh1, sc1, w_main, cos_t, sin_t, g_mla_q[l].reshape(1, -1),
            g_mla_kv[l].reshape(1, -1), w_mla_uq[l][:, uq_cols].astype(BF16), w_mla_ukv[l][:, ukv_cols].astype(BF16),
            n_lat)

        ya = _na(naq, nak, nav, _na_bias(na_rpb[l]), n_lat)

        lam_init = 0.8 - 0.6 * math.exp(-0.3 * l)
        dq = jnp.concatenate([_heads(dfqr, 2 * DIFF_HEADS), _heads(dfq, 2 * DIFF_HEADS)], axis=-1)
        dq = dq.reshape(b, DIFF_HEADS, 2, t, 2 * DIFF_DK).transpose(0, 1, 3, 2, 4).reshape(b, DIFF_HEADS, t, -1)
        dk = jnp.concatenate([_heads(dfkr, 2 * DIFF_HEADS), _heads(dfk, 2 * DIFF_HEADS)], axis=-1)
        dkt = dk.transpose(0, 1, 3, 2).reshape(b, DIFF_HEADS, 4 * DIFF_DK, t)
        yb = _tokens(_global_attention(
            functools.partial(_diff_kernel, lam_init=lam_init), "diff_attn", dq, dkt, _heads(dfv, DIFF_HEADS),
            (diff_lambda[l], g_diff[l].reshape(1, -1)), s_lat))

        mq = jnp.concatenate([_heads(mlqn, MLA_HEADS), _heads(mlqpr, MLA_HEADS), _heads(mlqp, MLA_HEADS)], axis=-1)
        kx = jnp.broadcast_to(mlkx[:, None, :, :2 * MLA_ROPE], (b, MLA_HEADS, t, 2 * MLA_ROPE))
        mkt = jnp.concatenate([_heads(mlkn, MLA_HEADS), kx], axis=-1).transpose(0, 1, 3, 2)
        ym = _tokens(_global_attention(_mla_kernel, "mla_attn", mq, mkt, _heads(mlv, MLA_HEADS), (), s_lat))

        ys = _s5_mixer(s5u, _s5_tables(s5_lam_re[l], s5_lam_im[l], s5_log_step[l], s5_b_re[l], s5_b_im[l],
                                       s5_c_re[l], s5_c_im[l]), s_lat)

        wr = jnp.concatenate([w_router[l], jnp.zeros((d, LANES - N_EXPERTS), F32)], axis=1)
        eb = jnp.concatenate([e_bias[l].astype(F32), jnp.full((LANES - N_EXPERTS,), -jnp.inf, F32)]).reshape(1, LANES)
        x_all, hp, idx, gw = _merge(
            ya, yb, ym, ys, s5u, s5_d[l].reshape(1, -1), w_glu[l], b_glu[l].reshape(1, -1), gates,
            w_branch[l].astype(BF16), w_out[l].astype(BF16), x_all, gt1, g_norm2[l].reshape(1, d), sh2, sc2, wr, eb,
            n_lat)

        n = b * t
        tok, blk_e, pos = _dispatch(idx.reshape(n, LANES)[:, :TOP_K])
        hp_flat = hp.reshape(n, d // 2)
        xs = _gather_rows(tok, hp_flat)
        y_sorted = _expert_ffn(blk_e, xs, w_e1[l].astype(BF16), w_e3[l].astype(BF16), w_e2[l].astype(BF16))
        x_all = _combine(pos, gw.reshape(n, LANES), y_sorted, hp_flat, x_all.reshape(n, d), gt2,
                         w_s1[l].astype(BF16), w_s3[l].astype(BF16), w_s2[l].astype(BF16),
                         tiles_per_batch, n_lat).reshape(b, t, d)

    return _final_norm(x_all, g_final, s_lat)
```

```python
import functools
import math

import jax
import jax.numpy as jnp
import numpy as np
from jax import lax
from jax.experimental import pallas as pl
from jax.experimental.pallas import tpu as pltpu

F32 = jnp.float32
BF16 = jnp.bfloat16
I32 = jnp.int32
HIGHEST = lax.Precision.HIGHEST

GRID_W = 64
HEAD_DIM = 64
NA_HEADS = 4
WIN_H = 8
WIN_W = 16
DIFF_HEADS = 4
DIFF_DK = 32
DIFF_DV = 64
MLA_HEADS = 4
MLA_NOPE = 64
MLA_ROPE = 32
MLA_V = 64
S5_GROUPS = 16
S5_GROUP_CH = 16
S5_STATE = 64
S5_WIDTH = S5_GROUPS * S5_GROUP_CH
S5_CHUNK = 16
N_BRANCH = 4
N_EXPERTS = 64
TOP_K = 6
EXPERT_BLOCK = 256
ROUTED_SCALE = 1.0
ROPE_BASE = 10000.0
EPS = 1e-6
NEG_INF = -1e30

TM = 256
LANES = 128
HI_MASK = -65536

NA_SCALE = HEAD_DIM ** -0.5
DIFF_SCALE = DIFF_DK ** -0.5
MLA_SCALE = (MLA_NOPE + MLA_ROPE) ** -0.5

OFF_NA, OFF_DQ, OFF_DK, OFF_DV, OFF_CQ, OFF_CKV, OFF_KPE, OFF_U, OFF_GATE = (
    0, 768, 1024, 1280, 1536, 1792, 1920, 2048, 2304)


def _cparams(sem, vmem_mb=None):
    kw = dict(dimension_semantics=sem)
    if vmem_mb is not None:
        kw["vmem_limit_bytes"] = vmem_mb * 2 ** 20
    return pltpu.CompilerParams(**kw)


def _rms(x, g):
    return x * lax.rsqrt(jnp.mean(x * x, axis=-1, keepdims=True) + EPS) * g


def _rope(x, cos, sin_signed):
    w = x.shape[-1]
    lane = lax.broadcasted_iota(I32, x.shape, 1)
    partner = jnp.where((lane & 8) == 0, pltpu.roll(x, w - 8, 1), pltpu.roll(x, 8, 1))
    return x * cos + partner * sin_signed


def _pack_bf16_pair(x):
    n = x.shape[-1] // 2
    bits = lax.bitcast_convert_type(x.astype(BF16).astype(F32), I32)
    return lax.shift_right_logical(bits[:, :n], 16) | (bits[:, n:] & HI_MASK)


def _unpack_bf16_pair(p):
    lo = lax.bitcast_convert_type(lax.shift_left(p, 16), F32)
    hi = lax.bitcast_convert_type(p & HI_MASK, F32)
    return lo, hi


def _ada_kernel(c_ref, w_ref, b_ref, o_ref):
    c = c_ref[...]
    s = c * jax.nn.sigmoid(c)
    o_ref[0] = jnp.dot(s, w_ref[0], preferred_element_type=F32, precision=HIGHEST) + b_ref[0]


def _ada(c8, w_ada, b_ada):
    depth, d, n = w_ada.shape
    tn = 1536
    return pl.pallas_call(
        _ada_kernel,
        grid=(depth, n // tn),
        in_specs=[pl.BlockSpec((8, d), lambda l, j: (0, 0)),
                  pl.BlockSpec((1, d, tn), lambda l, j: (l, 0, j)),
                  pl.BlockSpec((1, 1, tn), lambda l, j: (l, 0, j))],
        out_specs=pl.BlockSpec((1, 8, tn), lambda l, j: (l, 0, j)),
        out_shape=jax.ShapeDtypeStruct((depth, 8, n), F32),
        compiler_params=_cparams(("arbitrary", "arbitrary"), 40),
        name="ada",
    )(c8, w_ada, b_ada.reshape(depth, 1, n))


def _proj_kernel(n_lat, x_ref, g_ref, sh_ref, sc_ref, w_ref, cos_ref, sin_ref, gq_ref, gkv_ref, wuq_ref, wukv_ref,
                 naq, nak, nav, dfq, dfqr, dfk, dfkr, dfv, mlqn, mlqp, mlqpr, mlkn, mlv, mlkx, s5u, gates):
    is_ctx = pl.program_id(1) >= n_lat
    m_lat = jnp.where(is_ctx, 0.0, 1.0).astype(F32)
    m_ctx = 1.0 - m_lat
    h = (_rms(x_ref[0], g_ref[...]) * (1.0 + sc_ref[0]) + sh_ref[0]).astype(BF16)

    def proj(off, width):
        return jnp.dot(h, w_ref[:, off:off + width], preferred_element_type=F32)

    cos = cos_ref[...]
    sin = sin_ref[...]
    a = proj(OFF_NA, 768)
    naq[0] = (a[:, :256] * NA_SCALE).astype(BF16)
    nak[0] = a[:, 256:512].astype(BF16)
    nav[0] = a[:, 512:].astype(BF16)

    bq = proj(OFF_DQ, 256) * DIFF_SCALE
    dfq[0] = bq.astype(BF16)
    dfqr[0] = _rope(bq, cos, sin).astype(BF16)
    bk = proj(OFF_DK, 256)
    dfk[0] = (bk * m_ctx).astype(BF16)
    dfkr[0] = (_rope(bk, cos, sin) * m_lat).astype(BF16)
    dfv[0] = proj(OFF_DV, 256).astype(BF16)

    cq = _rms(proj(OFF_CQ, 256), gq_ref[...]).astype(BF16)
    q2 = jnp.dot(cq, wuq_ref[...], preferred_element_type=F32) * MLA_SCALE
    qp = q2[:, 256:]
    mlqn[0] = q2[:, :256].astype(BF16)
    mlqp[0] = qp.astype(BF16)
    mlqpr[0] = _rope(qp, cos[:, :LANES], sin[:, :LANES]).astype(BF16)
    ckv = _rms(proj(OFF_CKV, 128), gkv_ref[...]).astype(BF16)
    kv = jnp.dot(ckv, wukv_ref[...], preferred_element_type=F32)
    mlkn[0] = kv[:, :256].astype(BF16)
    mlv[0] = kv[:, 256:].astype(BF16)
    kpe = proj(OFF_KPE, LANES)
    kper = _rope(kpe, cos[:, :LANES], sin[:, :LANES])
    mlkx[0] = (kper * m_lat + pltpu.roll(kpe, MLA_ROPE, 1) * m_ctx).astype(BF16)

    s5u[0] = proj(OFF_U, 256)
    d = x_ref.shape[-1]
    for i in range(N_BRANCH):
        gates[0, :, i * d:(i + 1) * d] = jax.nn.sigmoid(proj(OFF_GATE + i * d, d)).astype(BF16)


def _proj(x_all, g1, shift, scale, w_main, cos_t, sin_t, g_q, g_kv, w_uq, w_ukv, n_lat):
    b, t, d = x_all.shape
    nt = t // TM
    mod_spec = pl.BlockSpec((1, 1, d), lambda bi, i: (bi * 2 + (i >= n_lat).astype(I32), 0, 0))

    def full(a):
        return pl.BlockSpec(a.shape, lambda bi, i: (0,) * a.ndim)

    def tile(w):
        return pl.BlockSpec((1, TM, w), lambda bi, i: (bi, i, 0))

    widths = [256, 256, 256, 256, 256, 256, 256, 256, 256, 128, 128, 256, 256, 128, 256, N_BRANCH * d]
    dtypes = [BF16] * 14 + [F32, BF16]
    return pl.pallas_call(
        functools.partial(_proj_kernel, n_lat),
        grid=(b, nt),
        in_specs=[tile(d), full(g1), mod_spec, mod_spec, full(w_main),
                  pl.BlockSpec((TM, 256), lambda bi, i: (i, 0)), pl.BlockSpec((TM, 256), lambda bi, i: (i, 0)),
                  full(g_q), full(g_kv), full(w_uq), full(w_ukv)],
        out_specs=[tile(w) for w in widths],
        out_shape=[jax.ShapeDtypeStruct((b, t, w), dt) for w, dt in zip(widths, dtypes)],
        compiler_params=_cparams(("arbitrary", "arbitrary"), 56),
        name="proj",
    )(x_all, g1, shift, scale, w_main, cos_t, sin_t, g_q, g_kv, w_uq, w_ukv)


def _na_kernel(q_ref, k0, k1, k2, kc, v0, v1, v2, vc, bias_ref, o_ref):
    nt_dims = (((1,), (1,)), ((), ()))
    outs = []
    for h in range(NA_HEADS):
        sl = slice(h * HEAD_DIM, (h + 1) * HEAD_DIM)
        qh = q_ref[0, :, sl]
        sw = jnp.concatenate(
            [lax.dot_general(qh, kr[0, :, sl], nt_dims, preferred_element_type=F32) for kr in (k0, k1, k2)],
            axis=1) + bias_ref[0, h]
        sc = lax.dot_general(qh, kc[0, :, sl], nt_dims, preferred_element_type=F32)
        m = jnp.maximum(jnp.max(sw, axis=1, keepdims=True), jnp.max(sc, axis=1, keepdims=True))
        pw = jnp.exp(sw - m)
        pc = jnp.exp(sc - m)
        l = jnp.sum(pw, axis=1, keepdims=True) + jnp.sum(pc, axis=1, keepdims=True)
        o = jnp.dot(pc.astype(BF16), vc[0, :, sl], preferred_element_type=F32)
        for j, vr in enumerate((v0, v1, v2)):
            o = o + jnp.dot(pw[:, j * TM:(j + 1) * TM].astype(BF16), vr[0, :, sl], preferred_element_type=F32)
        outs.append(o / l)
    o_ref[0] = jnp.concatenate(outs, axis=1).astype(BF16)


def _na_bias(rpb):
    rows_q = TM // GRID_W
    rows_k = 3 * rows_q
    qr = np.arange(rows_q)
    kr = np.arange(rows_k)
    col = np.arange(GRID_W)
    cstart = np.clip(col - WIN_W // 2, 0, GRID_W - WIN_W)
    col_ok = (col[None, :] >= cstart[:, None]) & (col[None, :] < cstart[:, None] + WIN_W)
    dc = np.clip(col[None, :] - col[:, None], 1 - WIN_W, WIN_W - 1) + (WIN_W - 1)
    sel_c = (dc[:, :, None] == np.arange(2 * WIN_W - 1)).astype(np.float32)
    kinds = []
    for start_rel, q_rel in ((np.zeros_like(qr), qr), (qr, qr + 4), (np.full_like(qr, 4), qr + 8)):
        row_ok = (kr[None, :] >= start_rel[:, None]) & (kr[None, :] < start_rel[:, None] + WIN_H)
        dr = np.clip(kr[None, :] - q_rel[:, None] + (WIN_H - 1), 0, 2 * WIN_H - 2)
        sel_r = (dr[:, :, None] == np.arange(2 * WIN_H - 1)).astype(np.float32)
        bias = jnp.einsum('hab,rsa,cdb->hrcsd', rpb.astype(F32), sel_r, sel_c, precision=HIGHEST)
        ok = row_ok[:, None, :, None] & col_ok[None, :, None, :]
        kinds.append(jnp.where(ok[None], bias, NEG_INF).reshape(NA_HEADS, TM, 3 * TM))
    kinds.append(jnp.full((NA_HEADS, TM, 3 * TM), NEG_INF, F32))
    return jnp.stack(kinds)


def _na(q, k, v, bias, n_lat):
    b, t, w = q.shape
    nt = t // TM

    def base(i):
        return jnp.clip(i - 1, 0, n_lat - 3)

    def kind(i):
        return jnp.where(i >= n_lat, 3, jnp.where(i == 0, 0, jnp.where(i == n_lat - 1, 2, 1)))

    def win(j):
        return pl.BlockSpec((1, TM, w), lambda bi, i: (bi, base(i) + j, 0))

    ctx = pl.BlockSpec((1, TM, w), lambda bi, i: (bi, n_lat, 0))
    own = pl.BlockSpec((1, TM, w), lambda bi, i: (bi, i, 0))
    return pl.pallas_call(
        _na_kernel,
        grid=(b, nt),
        in_specs=[own, win(0), win(1), win(2), ctx, win(0), win(1), win(2), ctx,
                  pl.BlockSpec((1, NA_HEADS, TM, 3 * TM), lambda bi, i: (kind(i), 0, 0, 0))],
        out_specs=own,
        out_shape=jax.ShapeDtypeStruct((b, t, w), BF16),
        compiler_params=_cparams(("arbitrary", "arbitrary"), 40),
        name="na_attn",
    )(q, k, k, k, k, v, v, v, v, bias)


def _softmax_step(q, kt, v1, carry):
    m, acc = carry
    s = jnp.dot(q, kt, preferred_element_type=F32)
    m_new = jnp.maximum(m, jnp.max(s, axis=1, keepdims=True))
    p = jnp.exp(s - m_new).astype(BF16)
    acc = jnp.exp(m - m_new) * acc + jnp.dot(p, v1, preferred_element_type=F32)
    return m_new, acc


def _attend(n_maps, s_lat, tk, q_ref, kt_ref, v_ref, emit):
    tq = q_ref.shape[2]
    t = v_ref.shape[2]
    dv = v_ref.shape[3] // 2
    dk = q_ref.shape[3] // n_maps
    is_ctx = pl.program_id(2) * tq >= s_lat
    qs = [q_ref[0, 0, :, m * dk:(m + 1) * dk] for m in range(n_maps)]
    init = tuple((jnp.full((tq, 1), NEG_INF, F32), jnp.zeros((tq, 2 * dv), F32)) for _ in range(n_maps))

    def body(j, carry):
        off = pl.multiple_of(j * tk, tk)
        v1 = v_ref[0, 0, pl.ds(off, tk), :]
        return tuple(_softmax_step(qs[m], kt_ref[0, 0, m * dk:(m + 1) * dk, pl.ds(off, tk)], v1, carry[m])
                     for m in range(n_maps))

    def finish(carry):
        vc = v_ref[0, 0, s_lat:t, :]
        outs = []
        for m in range(n_maps):
            _, acc = _softmax_step(qs[m], kt_ref[0, 0, m * dk:(m + 1) * dk, s_lat:t], vc, carry[m])
            outs.append(acc[:, :dv] / acc[:, dv:dv + 1])
        emit(outs)

    @pl.when(jnp.logical_not(is_ctx))
    def _():
        n_steps = s_lat // tk
        finish(lax.fori_loop(0, n_steps, body, init, unroll=min(2, n_steps)))

    @pl.when(is_ctx)
    def _():
        finish(init)


def _mla_kernel(s_lat, tk, q_ref, kt_ref, v_ref, o_ref):
    def emit(outs):
        o_ref[0, 0] = outs[0].astype(BF16)

    _attend(1, s_lat, tk, q_ref, kt_ref, v_ref, emit)


def _diff_kernel(s_lat, tk, q_ref, kt_ref, v_ref, lam_ref, g_ref, o_ref, *, lam_init):
    def emit(outs):
        lam = lam_ref[...]
        lam_full = (jnp.exp(jnp.sum(lam[0:1] * lam[1:2], axis=1, keepdims=True))
                    - jnp.exp(jnp.sum(lam[2:3] * lam[3:4], axis=1, keepdims=True)) + lam_init)
        d = _rms(outs[0] - lam_full * outs[1], g_ref[...]) * (1.0 - lam_init)
        o_ref[0, 0] = d.astype(BF16)

    _attend(2, s_lat, tk, q_ref, kt_ref, v_ref, emit)


def _with_ones(v):
    pad = jnp.zeros(v.shape[:-1] + (v.shape[-1] - 1,), v.dtype)
    return jnp.concatenate([v, jnp.ones(v.shape[:-1] + (1,), v.dtype), pad], axis=-1)


def _global_attention(kernel, name, q, kt, v, extra, s_lat):
    b, h, t, dq = q.shape
    dv = v.shape[-1]
    v = _with_ones(v)
    tk = min(1024, s_lat)
    in_specs = [pl.BlockSpec((1, 1, TM, dq), lambda bi, hi, i: (bi, hi, i, 0)),
                pl.BlockSpec((1, 1, kt.shape[2], t), lambda bi, hi, i: (bi, hi, 0, 0)),
                pl.BlockSpec((1, 1, t, 2 * dv), lambda bi, hi, i: (bi, hi, 0, 0))]
    in_specs += [pl.BlockSpec(a.shape, lambda bi, hi, i: (0, 0)) for a in extra]
    return pl.pallas_call(
        functools.partial(kernel, s_lat, tk),
        grid=(b, h, t // TM),
        in_specs=in_specs,
        out_specs=pl.BlockSpec((1, 1, TM, dv), lambda bi, hi, i: (bi, hi, i, 0)),
        out_shape=jax.ShapeDtypeStruct((b, h, t, dv), BF16),
        compiler_params=_cparams(("arbitrary", "arbitrary", "arbitrary"), 48),
        name=name,
    )(q, kt, v, *extra)


def _s5_tables(lam_re, lam_im, log_step, b_re, b_im, c_re, c_im):
    c = S5_CHUNK
    lr, li = lam_re.astype(F32), lam_im.astype(F32)
    step = jnp.exp(log_step.astype(F32))[..., None]
    mag = jnp.exp(lr * step)
    ar, ai = mag * jnp.cos(li * step), mag * jnp.sin(li * step)
    den = lr * lr + li * li
    fr = ((ar - 1.0) * lr + ai * li) / den
    fi = (ai * lr - (ar - 1.0) * li) / den
    br, bi = b_re.astype(F32), b_im.astype(F32)
    bbr = fr[..., None] * br - fi[..., None] * bi
    bbi = fr[..., None] * bi + fi[..., None] * br
    tau = jnp.arange(c + 1, dtype=F32)[:, None, None, None]
    pmag = jnp.exp(lr * step * tau)
    pr, pi = pmag * jnp.cos(li * step * tau), pmag * jnp.sin(li * step * tau)
    cr, ci = c_re.astype(F32), c_im.astype(F32)
    abr = pr[..., None] * bbr - pi[..., None] * bbi
    abi = pr[..., None] * bbi + pi[..., None] * bbr
    kk = jnp.einsum('dgip,tdgpj->tdgij', cr, abr) - jnp.einsum('dgip,tdgpj->tdgij', ci, abi)
    s_idx = jnp.arange(c)[:, None]
    t_idx = jnp.arange(c)[None, :]
    lag_f = jnp.clip(t_idx - s_idx, 0, c)
    lag_b = jnp.clip(s_idx - t_idx, 0, c)
    tf = jnp.where((s_idx <= t_idx)[:, :, None, None, None], kk[lag_f, 0], 0.0)
    tb = jnp.where((s_idx >= t_idx)[:, :, None, None, None], kk[lag_b, 1], 0.0)
    g, j_ch = S5_GROUPS, S5_GROUP_CH
    tmat = (tf + tb).transpose(2, 0, 4, 1, 3).reshape(g, c * j_ch, c * j_ch)
    pow_f = jnp.arange(c - 1, -1, -1)
    pow_b = jnp.arange(c)

    def exit_map(d, pw):
        re = abr[pw, d].transpose(1, 0, 3, 2).reshape(g, c * j_ch, S5_STATE)
        im = abi[pw, d].transpose(1, 0, 3, 2).reshape(g, c * j_ch, S5_STATE)
        return jnp.concatenate([re, im], axis=-1)

    bsum = jnp.stack([exit_map(0, pow_f), exit_map(1, pow_b)])

    def entry_map(d, pw):
        p_r, p_i = pr[pw, d], pi[pw, d]
        on_re = jnp.einsum('gip,tgp->gpti', cr[d], p_r) - jnp.einsum('gip,tgp->gpti', ci[d], p_i)
        on_im = -jnp.einsum('gip,tgp->gpti', cr[d], p_i) - jnp.einsum('gip,tgp->gpti', ci[d], p_r)
        return jnp.concatenate([on_re, on_im], axis=1).reshape(g, 2 * S5_STATE, c * j_ch)

    cin = jnp.stack([entry_map(0, jnp.arange(1, c + 1)), entry_map(1, jnp.arange(c, 0, -1))])
    a_c = (pr[c], pi[c])
    return tmat.astype(BF16), bsum.astype(BF16), cin.astype(BF16), a_c


def _s5_intra_kernel(u_ref, tm_ref, bs_ref, y_ref, sr_ref, si_ref):
    u = u_ref[0, 0]
    y_ref[0, 0] = jnp.dot(u, tm_ref[0], preferred_element_type=F32)
    for d in range(2):
        s = jnp.dot(u, bs_ref[d, 0], preferred_element_type=F32)
        sr_ref[0, d, 0] = s[:, :S5_STATE]
        si_ref[0, d, 0] = s[:, S5_STATE:]


def _s5_intra(ug, tmat, bsum):
    b, g, nc, w = ug.shape
    return pl.pallas_call(
        _s5_intra_kernel,
        grid=(b, g),
        in_specs=[pl.BlockSpec((1, 1, nc, w), lambda bi, gi: (bi, gi, 0, 0)),
                  pl.BlockSpec((1, w, w), lambda bi, gi: (gi, 0, 0)),
                  pl.BlockSpec((2, 1, w, 2 * S5_STATE), lambda bi, gi: (0, gi, 0, 0))],
        out_specs=[pl.BlockSpec((1, 1, nc, w), lambda bi, gi: (bi, gi, 0, 0)),
                   pl.BlockSpec((1, 2, 1, nc, S5_STATE), lambda bi, gi: (bi, 0, gi, 0, 0)),
                   pl.BlockSpec((1, 2, 1, nc, S5_STATE), lambda bi, gi: (bi, 0, gi, 0, 0))],
        out_shape=[jax.ShapeDtypeStruct((b, g, nc, w), F32),
                   jax.ShapeDtypeStruct((b, 2, g, nc, S5_STATE), F32),
                   jax.ShapeDtypeStruct((b, 2, g, nc, S5_STATE), F32)],
        compiler_params=_cparams(("arbitrary", "arbitrary")),
        name="s5_intra",
    )(ug, tmat, bsum)


def _s5_scan_kernel(sr_ref, si_ref, ar_ref, ai_ref, hr_ref, hi_ref, st_r, st_i):
    @pl.when(pl.program_id(0) == 0)
    def _():
        st_r[...] = jnp.zeros_like(st_r)
        st_i[...] = jnp.zeros_like(st_i)

    ar = ar_ref[...]
    ai = ai_ref[...]

    def body(k, carry):
        hr, hi = carry
        hr_ref[k] = hr
        hi_ref[k] = hi
        return ar * hr - ai * hi + sr_ref[k], ar * hi + ai * hr + si_ref[k]

    hr, hi = lax.fori_loop(0, sr_ref.shape[0], body, (st_r[...], st_i[...]))
    st_r[...] = hr
    st_i[...] = hi


def _s5_scan(sr, si, ar, ai):
    n, r, p = sr.shape
    kb = next(c for c in (104, 80, 64, 40, 16, 8, 4, 2, 1) if n % c == 0)
    blk = pl.BlockSpec((kb, r, p), lambda i: (i, 0, 0))
    par = pl.BlockSpec((r, p), lambda i: (0, 0))
    return pl.pallas_call(
        _s5_scan_kernel,
        grid=(n // kb,),
        in_specs=[blk, blk, par, par],
        out_specs=[blk, blk],
        out_shape=[jax.ShapeDtypeStruct((n, r, p), F32)] * 2,
        scratch_shapes=[pltpu.VMEM((r, p), F32), pltpu.VMEM((r, p), F32)],
        compiler_params=_cparams(("arbitrary",), 40),
        name="s5_scan",
    )(sr, si, ar, ai)


def _s5_inter_kernel(y_ref, h_ref, cin_ref, o_ref):
    y = y_ref[0, 0]
    for d in range(2):
        y = y + jnp.dot(h_ref[0, d, 0], cin_ref[d, 0], preferred_element_type=F32)
    o_ref[0, 0] = y


def _s5_inter(y_intra, hin, cin):
    b, g, nc, w = y_intra.shape
    return pl.pallas_call(
        _s5_inter_kernel,
        grid=(b, g),
        in_specs=[pl.BlockSpec((1, 1, nc, w), lambda bi, gi: (bi, gi, 0, 0)),
                  pl.BlockSpec((1, 2, 1, nc, 2 * S5_STATE), lambda bi, gi: (bi, 0, gi, 0, 0)),
                  pl.BlockSpec((2, 1, 2 * S5_STATE, w), lambda bi, gi: (0, gi, 0, 0))],
        out_specs=pl.BlockSpec((1, 1, nc, w), lambda bi, gi: (bi, gi, 0, 0)),
        out_shape=jax.ShapeDtypeStruct((b, g, nc, w), F32),
        compiler_params=_cparams(("arbitrary", "arbitrary")),
        name="s5_inter",
    )(y_intra, hin, cin)


def _s5_mixer(u, tables, s_lat):
    tmat, bsum, cin, (acr, aci) = tables
    b, t, w = u.shape
    c, g, jc = S5_CHUNK, S5_GROUPS, S5_GROUP_CH
    nc, ncl = t // c, s_lat // c
    ug = u.reshape(b, nc, c, g, jc).transpose(0, 3, 1, 2, 4).reshape(b, g, nc, c * jc).astype(BF16)
    y_intra, sr, si = _s5_intra(ug, tmat, bsum)

    def to_seq(s):
        fwd = jnp.concatenate([s[:, 0, :, ncl:], s[:, 0, :, :ncl]], axis=2)
        bwd = jnp.concatenate([s[:, 1, :, ncl:][:, :, ::-1], s[:, 1, :, :ncl][:, :, ::-1]], axis=2)
        return jnp.stack([fwd, bwd]).transpose(3, 0, 1, 2, 4).reshape(nc, 2 * b * g, S5_STATE)

    def from_seq(h):
        h = h.reshape(nc, 2, b, g, S5_STATE).transpose(1, 2, 3, 0, 4)
        ncc = nc - ncl
        fwd = jnp.concatenate([h[0][:, :, ncc:], h[0][:, :, :ncc]], axis=2)
        bwd = jnp.concatenate([h[1][:, :, ncc:][:, :, ::-1], h[1][:, :, :ncc][:, :, ::-1]], axis=2)
        return jnp.stack([fwd, bwd], axis=1)

    def rows(a):
        return jnp.broadcast_to(a[:, None], (2, b, g, S5_STATE)).reshape(2 * b * g, S5_STATE)

    hr, hi = _s5_scan(to_seq(sr), to_seq(si), rows(acr), rows(aci))
    hin = jnp.concatenate([from_seq(hr), from_seq(hi)], axis=-1).astype(BF16)
    y = _s5_inter(y_intra, hin, cin)
    return y.reshape(b, g, nc, c, jc).transpose(0, 2, 3, 1, 4).reshape(b, t, w)


def _merge_kernel(ya, yb, ym, ys, su, dd, wglu, bglu, gates, wb, wo, x_ref, gt1, g2, sh2, sc2, wr, eb,
                  xo_ref, hp_ref, idx_ref, gw_ref, hist_ref):
    d = x_ref.shape[-1]
    y5 = ys[0] + dd[...] * su[0]
    gl = jax.nn.gelu(y5)
    yd = gl * jax.nn.sigmoid(jnp.dot(gl, wglu[...], preferred_element_type=F32) + bglu[...])
    branches = (ya[0], yb[0], ym[0], yd.astype(BF16))
    m = None
    for i in range(N_BRANCH):
        term = gates[0, :, i * d:(i + 1) * d].astype(F32) * jnp.dot(branches[i], wb[i], preferred_element_type=F32)
        m = term if m is None else m + term
    xn = x_ref[0] + gt1[0] * jnp.dot(m.astype(BF16), wo[...], preferred_element_type=F32)
    xo_ref[0] = xn
    h2 = _rms(xn, g2[...]) * (1.0 + sc2[0]) + sh2[0]
    hp_ref[0] = _pack_bf16_pair(h2)

    scores = jax.nn.sigmoid(jnp.dot(h2, wr[...], preferred_element_type=F32, precision=HIGHEST))
    sel = scores + eb[...]
    lane = lax.broadcasted_iota(I32, sel.shape, 1).astype(F32)
    idx_acc = jnp.zeros(sel.shape, F32)
    gw_acc = jnp.zeros(sel.shape, F32)
    chosen = jnp.zeros(sel.shape, F32)
    for k in range(TOP_K):
        mx = jnp.max(sel, axis=1, keepdims=True)
        ik = jnp.min(jnp.where(sel == mx, lane, float(LANES)), axis=1, keepdims=True)
        hit = lane == ik
        gk = jnp.sum(jnp.where(hit, scores, 0.0), axis=1, keepdims=True)
        idx_acc = jnp.where(lane == k, ik, idx_acc)
        gw_acc = jnp.where(lane == k, gk, gw_acc)
        chosen = jnp.where(hit, 1.0, chosen)
        sel = jnp.where(hit, -jnp.inf, sel)
    idx_ref[0] = idx_acc.astype(I32)
    gw_ref[0] = gw_acc / jnp.sum(gw_acc, axis=1, keepdims=True) * ROUTED_SCALE
    hist_ref[0, 0] = jnp.sum(chosen, axis=0, keepdims=True)


def _merge(ya, yb, ym, ys, su, dd, wglu, bglu, gates, wb, wo, x_all, gt1, g2, sh2, sc2, wr, eb, n_lat):
    b, t, d = x_all.shape
    mod_spec = pl.BlockSpec((1, 1, d), lambda bi, i: (bi * 2 + (i >= n_lat).astype(I32), 0, 0))

    def full(a):
        return pl.BlockSpec(a.shape, lambda bi, i: (0,) * a.ndim)

    def tile(w):
        return pl.BlockSpec((1, TM, w), lambda bi, i: (bi, i, 0))

    return pl.pallas_call(
        _merge_kernel,
        grid=(b, t // TM),
        in_specs=[tile(256), tile(256), tile(256), tile(256), tile(256), full(dd), full(wglu), full(bglu),
                  tile(N_BRANCH * d), full(wb), full(wo), tile(d), mod_spec, full(g2), mod_spec, mod_spec,
                  full(wr), full(eb)],
        out_specs=[tile(d), tile(d // 2), tile(LANES), tile(LANES),
                   pl.BlockSpec((1, 1, 1, LANES), lambda bi, i: (bi, i, 0, 0))],
        out_shape=[jax.ShapeDtypeStruct((b, t, d), F32), jax.ShapeDtypeStruct((b, t, d // 2), I32),
                   jax.ShapeDtypeStruct((b, t, LANES), I32), jax.ShapeDtypeStruct((b, t, LANES), F32),
                   jax.ShapeDtypeStruct((b, t // TM, 1, LANES), F32)],
        compiler_params=_cparams(("arbitrary", "arbitrary"), 48),
        name="merge",
    )(ya, yb, ym, ys, su, dd, wglu, bglu, gates, wb, wo, x_all, gt1, g2, sh2, sc2, wr, eb)


def _route_tables(hist, n_assign):
    h = hist.reshape(-1, LANES).astype(I32)
    counts = jnp.sum(h, axis=0)
    padded = (counts + EXPERT_BLOCK - 1) // EXPERT_BLOCK * EXPERT_BLOCK
    pends = jnp.cumsum(padded)
    base = (pends - padded)[None, :] + jnp.cumsum(h, axis=0) - h
    n_blocks = -(-n_assign // EXPERT_BLOCK) + N_EXPERTS
    block_start = jnp.arange(n_blocks, dtype=I32) * EXPERT_BLOCK
    blk_e = jnp.minimum(jnp.sum((pends[None, :N_EXPERTS] <= block_start[:, None]).astype(I32), axis=1),
                        N_EXPERTS - 1)
    return base.astype(F32).reshape(hist.shape), blk_e, n_blocks


def _pos_kernel(idx_ref, base_ref, pos_ref):
    idx = idx_ref[0]
    lane = lax.broadcasted_iota(I32, idx.shape, 1)
    hits = [lane == idx[:, k:k + 1] for k in range(TOP_K)]
    chosen = jnp.zeros(idx.shape, F32)
    for hit in hits:
        chosen = jnp.where(hit, 1.0, chosen)
    row = lax.broadcasted_iota(I32, (TM, TM), 0)
    col = lax.broadcasted_iota(I32, (TM, TM), 1)
    earlier = jnp.where(row > col, 1.0, 0.0).astype(BF16)
    slot = jnp.dot(earlier, chosen.astype(BF16), preferred_element_type=F32) + base_ref[0, 0]
    pos = jnp.zeros(idx.shape, F32)
    for k, hit in enumerate(hits):
        pos = jnp.where(lane == k, jnp.sum(jnp.where(hit, slot, 0.0), axis=1, keepdims=True), pos)
    pos_ref[0] = pos.astype(I32)


def _positions(idx, base):
    b, t, _ = idx.shape
    tile = pl.BlockSpec((1, TM, LANES), lambda bi, i: (bi, i, 0))
    return pl.pallas_call(
        _pos_kernel,
        grid=(b, t // TM),
        in_specs=[tile, pl.BlockSpec((1, 1, 1, LANES), lambda bi, i: (bi, i, 0, 0))],
        out_specs=tile,
        out_shape=jax.ShapeDtypeStruct((b, t, LANES), I32),
        compiler_params=_cparams(("arbitrary", "arbitrary")),
        name="moe_pos",
    )(idx, base)


DMA_UNROLL = 8


def _scatter_kernel(pos_ref, hp_ref, init_ref, xs_ref, sem):
    del init_ref
    for k in range(TOP_K):
        def issue(r, c, k=k):
            pltpu.make_async_copy(hp_ref.at[pl.ds(r, 1)], xs_ref.at[pl.ds(pos_ref[0, 0, k * TM + r], 1)],
                                  sem).start()
            return c
        lax.fori_loop(0, TM, issue, 0, unroll=DMA_UNROLL)
    for k in range(TOP_K):
        pltpu.make_async_copy(hp_ref, xs_ref.at[pl.ds(0, TM)], sem).wait()


def _tile_major(pos):
    nt = pos.shape[0] // TM
    return pos[:, :TOP_K].reshape(nt, TM, TOP_K).transpose(0, 2, 1).reshape(nt, 1, TOP_K * TM)


def _scatter_rows(pos3, hp, n_slots):
    n, half = hp.shape
    return pl.pallas_call(
        _scatter_kernel,
        grid=(n // TM,),
        in_specs=[pl.BlockSpec((1, 1, TOP_K * TM), lambda i: (i, 0, 0), memory_space=pltpu.SMEM),
                  pl.BlockSpec((TM, half), lambda i: (i, 0)),
                  pl.BlockSpec(memory_space=pl.ANY)],
        out_specs=pl.BlockSpec(memory_space=pl.ANY),
        out_shape=jax.ShapeDtypeStruct((n_slots, half), hp.dtype),
        scratch_shapes=[pltpu.SemaphoreType.DMA(())],
        input_output_aliases={2: 0},
        compiler_params=_cparams(("arbitrary",)),
        name="moe_scatter",
    )(pos3, hp, jnp.zeros((n_slots, half), hp.dtype))


def _swiglu_packed(p, w1, w3, w2):
    lo, hi = _unpack_bf16_pair(p)
    lo, hi = lo.astype(BF16), hi.astype(BF16)
    n = lo.shape[-1]

    def up(w):
        return (jnp.dot(lo, w[:n], preferred_element_type=F32) + jnp.dot(hi, w[n:], preferred_element_type=F32))

    a = up(w1)
    return jnp.dot((a * jax.nn.sigmoid(a) * up(w3)).astype(BF16), w2, preferred_element_type=F32)


def _ffn_kernel(be_ref, x_ref, w1_ref, w3_ref, w2_ref, y_ref):
    del be_ref
    y_ref[...] = _pack_bf16_pair(_swiglu_packed(x_ref[...], w1_ref[0], w3_ref[0], w2_ref[0]))


def _expert_ffn(blk_e, xs, w1, w3, w2):
    n_slots, half = xs.shape
    e, d, ff = w1.shape
    row = pl.BlockSpec((EXPERT_BLOCK, half), lambda i, be: (i, 0))
    return pl.pallas_call(
        _ffn_kernel,
        grid_spec=pltpu.PrefetchScalarGridSpec(
            num_scalar_prefetch=1,
            grid=(n_slots // EXPERT_BLOCK,),
            in_specs=[row,
                      pl.BlockSpec((1, d, ff), lambda i, be: (be[i], 0, 0)),
                      pl.BlockSpec((1, d, ff), lambda i, be: (be[i], 0, 0)),
                      pl.BlockSpec((1, ff, d), lambda i, be: (be[i], 0, 0))],
            out_specs=row),
        out_shape=jax.ShapeDtypeStruct((n_slots, half), I32),
        compiler_params=_cparams(("arbitrary",)),
        name="moe_ffn",
    )(blk_e, xs, w1, w3, w2)


def _combine_kernel(pos_ref, gw_ref, ys_ref, hp_ref, x_ref, gt2, ws1, ws3, ws2, xo_ref, buf, sem):
    for k in range(TOP_K):
        def issue(r, c, k=k):
            pltpu.make_async_copy(ys_ref.at[pl.ds(pos_ref[0, 0, k * TM + r], 1)], buf.at[k, pl.ds(r, 1)],
                                  sem).start()
            return c
        lax.fori_loop(0, TM, issue, 0, unroll=DMA_UNROLL)

    shared = _swiglu_packed(hp_ref[...], ws1[...], ws3[...], ws2[...])

    for k in range(TOP_K):
        pltpu.make_async_copy(ys_ref.at[pl.ds(0, TM)], buf.at[k], sem).wait()

    gw = gw_ref[...]
    lo = hi = None
    for k in range(TOP_K):
        wk = gw[:, k:k + 1]
        lk, hk = _unpack_bf16_pair(buf[k])
        lo = wk * lk if lo is None else lo + wk * lk
        hi = wk * hk if hi is None else hi + wk * hk
    xo_ref[...] = x_ref[...] + gt2[0] * (shared + jnp.concatenate([lo, hi], axis=1))


def _combine(pos3, gw, ys, hp, x_flat, gt2, ws1, ws3, ws2, tiles_per_batch, n_lat):
    n, d = x_flat.shape
    nt = n // TM

    def full(a):
        return pl.BlockSpec(a.shape, lambda i: (0,) * a.ndim)

    def tile(w):
        return pl.BlockSpec((TM, w), lambda i: (i, 0))

    def mod_idx(i):
        return (i // tiles_per_batch) * 2 + ((i % tiles_per_batch) >= n_lat).astype(I32)

    return pl.pallas_call(
        _combine_kernel,
        grid=(nt,),
        in_specs=[pl.BlockSpec((1, 1, TOP_K * TM), lambda i: (i, 0, 0), memory_space=pltpu.SMEM),
                  tile(LANES), pl.BlockSpec(memory_space=pl.ANY), tile(d // 2), tile(d),
                  pl.BlockSpec((1, 1, d), lambda i: (mod_idx(i), 0, 0)), full(ws1), full(ws3), full(ws2)],
        out_specs=tile(d),
        out_shape=jax.ShapeDtypeStruct((n, d), F32),
        scratch_shapes=[pltpu.VMEM((TOP_K, TM, d // 2), I32), pltpu.SemaphoreType.DMA(())],
        compiler_params=_cparams(("arbitrary",), 40),
        name="moe_combine",
    )(pos3, gw, ys, hp, x_flat, gt2, ws1, ws3, ws2)


def _final_kernel(x_ref, g_ref, o_ref):
    o_ref[0] = _rms(x_ref[0], g_ref[...])


def _final_norm(x_all, g, s_lat):
    b, _, d = x_all.shape
    return pl.pallas_call(
        _final_kernel,
        grid=(b, s_lat // TM),
        in_specs=[pl.BlockSpec((1, TM, d), lambda bi, i: (bi, i, 0)), pl.BlockSpec((1, d), lambda bi, i: (0, 0))],
        out_specs=pl.BlockSpec((1, TM, d), lambda bi, i: (bi, i, 0)),
        out_shape=jax.ShapeDtypeStruct((b, s_lat, d), F32),
        compiler_params=_cparams(("arbitrary", "arbitrary")),
        name="final_norm",
    )(x_all, g.reshape(1, d))


def _rope_tables(s_lat, t):
    pos = jnp.arange(s_lat, dtype=I32)
    half = 8
    freqs = ROPE_BASE ** (-jnp.arange(half, dtype=F32) * 2.0 / (2 * half))

    def part(p):
        ang = p.astype(F32)[:, None] * freqs[None, :]
        c, s = jnp.cos(ang), jnp.sin(ang)
        return jnp.concatenate([c, c], axis=1), jnp.concatenate([-s, s], axis=1)

    cr, sr = part(pos // GRID_W)
    cc, sc = part(pos % GRID_W)
    cos = jnp.tile(jnp.concatenate([cr, cc], axis=1), (1, 8))
    sin = jnp.tile(jnp.concatenate([sr, sc], axis=1), (1, 8))
    pad = t - s_lat
    return (jnp.concatenate([cos, jnp.ones((pad, 256), F32)], axis=0),
            jnp.concatenate([sin, jnp.zeros((pad, 256), F32)], axis=0))


def _heads(x, h):
    b, t, w = x.shape
    return x.reshape(b, t, h, w // h).transpose(0, 2, 1, 3)


def _tokens(o):
    b, h, t, d = o.shape
    return o.transpose(0, 2, 1, 3).reshape(b, t, h * d)


def kernel(x, c, ctx, c_ctx, w_ada, b_ada, g_norm1, g_norm2, w_in, na_rpb, diff_lambda, g_diff, g_mla_q, g_mla_kv, w_mla_uq, w_mla_ukv, s5_lam_re, s5_lam_im, s5_log_step, s5_b_re, s5_b_im, s5_c_re, s5_c_im, s5_d, w_glu, b_glu, w_branch, w_out, w_router, e_bias, w_e1, w_e3, w_e2, w_s1, w_s3, w_s2, g_final):
    b, s_lat, d = x.shape
    lc = ctx.shape[1]
    t = s_lat + lc
    depth = w_ada.shape[0]
    n_lat = s_lat // TM
    assert lc == TM and s_lat % TM == 0 and n_lat >= 3 and b + 1 <= 8 and d % 256 == 0
    tiles_per_batch = t // TM

    x_all = jnp.concatenate([x, ctx], axis=1)
    c8 = jnp.zeros((8, d), F32).at[:b].set(c).at[b].set(c_ctx)
    ada = _ada(c8, w_ada, b_ada)
    cos_t, sin_t = _rope_tables(s_lat, t)

    hq = jnp.arange(MLA_HEADS)[:, None] * (MLA_NOPE + MLA_ROPE)
    uq_cols = jnp.concatenate([(hq + jnp.arange(MLA_NOPE)[None]).reshape(-1),
                               (hq + MLA_NOPE + jnp.arange(MLA_ROPE)[None]).reshape(-1)])
    hk = jnp.arange(MLA_HEADS)[:, None] * (MLA_NOPE + MLA_V)
    ukv_cols = jnp.concatenate([(hk + jnp.arange(MLA_NOPE)[None]).reshape(-1),
                                (hk + MLA_NOPE + jnp.arange(MLA_V)[None]).reshape(-1)])
    kpe_end = 1952

    for l in range(depth):
        mods = jnp.concatenate([ada[l, :b, None, :], jnp.broadcast_to(ada[l, b], (b, 1, 6 * d))], axis=1)
        sh1, sc1, gt1, sh2, sc2, gt2 = (mods[:, :, i * d:(i + 1) * d].reshape(2 * b, 1, d) for i in range(6))
        w_main = jnp.concatenate([w_in[l][:, :kpe_end], jnp.zeros((d, OFF_U - kpe_end), F32),
                                  w_in[l][:, kpe_end:]], axis=1).astype(BF16)
        (naq, nak, nav, dfq, dfqr, dfk, dfkr, dfv, mlqn, mlqp, mlqpr, mlkn, mlv, mlkx, s5u, gates) = _proj(
            x_all, g_norm1[l].reshape(1, d), sh1, sc1, w_main, cos_t, sin_t, g_mla_q[l].reshape(1, -1),
            g_mla_kv[l].reshape(1, -1), w_mla_uq[l][:, uq_cols].astype(BF16), w_mla_ukv[l][:, ukv_cols].astype(BF16),
            n_lat)

        ya = _na(naq, nak, nav, _na_bias(na_rpb[l]), n_lat)

        lam_init = 0.8 - 0.6 * math.exp(-0.3 * l)
        dq = jnp.concatenate([_heads(dfqr, 2 * DIFF_HEADS), _heads(dfq, 2 * DIFF_HEADS)], axis=-1)
        dq = dq.reshape(b, DIFF_HEADS, 2, t, 2 * DIFF_DK).transpose(0, 1, 3, 2, 4).reshape(b, DIFF_HEADS, t, -1)
        dk = jnp.concatenate([_heads(dfkr, 2 * DIFF_HEADS), _heads(dfk, 2 * DIFF_HEADS)], axis=-1)
        dkt = dk.transpose(0, 1, 3, 2).reshape(b, DIFF_HEADS, 4 * DIFF_DK, t)
        yb = _tokens(_global_attention(
            functools.partial(_diff_kernel, lam_init=lam_init), "diff_attn", dq, dkt, _heads(dfv, DIFF_HEADS),
            (diff_lambda[l], g_diff[l].reshape(1, -1)), s_lat))

        mq = jnp.concatenate([_heads(mlqn, MLA_HEADS), _heads(mlqpr, MLA_HEADS), _heads(mlqp, MLA_HEADS)], axis=-1)
        kx = jnp.broadcast_to(mlkx[:, None, :, :2 * MLA_ROPE], (b, MLA_HEADS, t, 2 * MLA_ROPE))
        mkt = jnp.concatenate([_heads(mlkn, MLA_HEADS), kx], axis=-1).transpose(0, 1, 3, 2)
        ym = _tokens(_global_attention(_mla_kernel, "mla_attn", mq, mkt, _heads(mlv, MLA_HEADS), (), s_lat))

        ys = _s5_mixer(s5u, _s5_tables(s5_lam_re[l], s5_lam_im[l], s5_log_step[l], s5_b_re[l], s5_b_im[l],
                                       s5_c_re[l], s5_c_im[l]), s_lat)

        wr = jnp.concatenate([w_router[l], jnp.zeros((d, LANES - N_EXPERTS), F32)], axis=1)
        eb = jnp.concatenate([e_bias[l].astype(F32), jnp.full((LANES - N_EXPERTS,), -jnp.inf, F32)]).reshape(1, LANES)
        x_all, hp, idx, gw, hist = _merge(
            ya, yb, ym, ys, s5u, s5_d[l].reshape(1, -1), w_glu[l], b_glu[l].reshape(1, -1), gates,
            w_branch[l].astype(BF16), w_out[l].astype(BF16), x_all, gt1, g_norm2[l].reshape(1, d), sh2, sc2, wr, eb,
            n_lat)

        n = b * t
        base, blk_e, n_blocks = _route_tables(hist, n * TOP_K)
        pos3 = _tile_major(_positions(idx, base).reshape(n, LANES))
        hp_flat = hp.reshape(n, d // 2)
        xs = _scatter_rows(pos3, hp_flat, n_blocks * EXPERT_BLOCK)
        y_sorted = _expert_ffn(blk_e, xs, w_e1[l].astype(BF16), w_e3[l].astype(BF16), w_e2[l].astype(BF16))
        x_all = _combine(pos3, gw.reshape(n, LANES), y_sorted, hp_flat, x_all.reshape(n, d), gt2,
                         w_s1[l].astype(BF16), w_s3[l].astype(BF16), w_s2[l].astype(BF16),
                         tiles_per_batch, n_lat).reshape(b, t, d)

    return _final_norm(x_all, g_final, s_lat)
```

```python
import functools
import math

import jax
import jax.numpy as jnp
import numpy as np
from jax import lax
from jax.experimental import pallas as pl
from jax.experimental.pallas import tpu as pltpu

F32 = jnp.float32
BF16 = jnp.bfloat16
I32 = jnp.int32
HIGHEST = lax.Precision.HIGHEST

GRID_W = 64
HEAD_DIM = 64
NA_HEADS = 4
WIN_H = 8
WIN_W = 16
DIFF_HEADS = 4
DIFF_DK = 32
DIFF_DV = 64
MLA_HEADS = 4
MLA_NOPE = 64
MLA_ROPE = 32
MLA_V = 64
S5_GROUPS = 16
S5_GROUP_CH = 16
S5_STATE = 64
S5_WIDTH = S5_GROUPS * S5_GROUP_CH
S5_CHUNK = 16
N_BRANCH = 4
N_EXPERTS = 64
TOP_K = 6
EXPERT_BLOCK = 256
ROUTED_SCALE = 1.0
ROPE_BASE = 10000.0
EPS = 1e-6
NEG_INF = -1e30

TM = 256
LANES = 128
HI_MASK = -65536

NA_SCALE = HEAD_DIM ** -0.5
DIFF_SCALE = DIFF_DK ** -0.5
MLA_SCALE = (MLA_NOPE + MLA_ROPE) ** -0.5

OFF_NA, OFF_DQ, OFF_DK, OFF_DV, OFF_CQ, OFF_CKV, OFF_KPE, OFF_U, OFF_GATE = (
    0, 768, 1024, 1280, 1536, 1792, 1920, 2048, 2304)


def _cparams(sem, vmem_mb=None):
    kw = dict(dimension_semantics=sem)
    if vmem_mb is not None:
        kw["vmem_limit_bytes"] = vmem_mb * 2 ** 20
    return pltpu.CompilerParams(**kw)


def _rms(x, g):
    return x * lax.rsqrt(jnp.mean(x * x, axis=-1, keepdims=True) + EPS) * g


def _rope(x, cos, sin_signed):
    w = x.shape[-1]
    lane = lax.broadcasted_iota(I32, x.shape, 1)
    partner = jnp.where((lane & 8) == 0, pltpu.roll(x, w - 8, 1), pltpu.roll(x, 8, 1))
    return x * cos + partner * sin_signed


def _pack_bf16_pair(x):
    n = x.shape[-1] // 2
    bits = lax.bitcast_convert_type(x.astype(BF16).astype(F32), I32)
    return lax.shift_right_logical(bits[:, :n], 16) | (bits[:, n:] & HI_MASK)


def _unpack_bf16_pair(p):
    lo = lax.bitcast_convert_type(lax.shift_left(p, 16), F32)
    hi = lax.bitcast_convert_type(p & HI_MASK, F32)
    return lo, hi


def _ada_kernel(c_ref, w_ref, b_ref, o_ref):
    c = c_ref[...]
    s = c * jax.nn.sigmoid(c)
    o_ref[0] = jnp.dot(s, w_ref[0], preferred_element_type=F32, precision=HIGHEST) + b_ref[0]


def _ada(c8, w_ada, b_ada):
    depth, d, n = w_ada.shape
    tn = 1536
    return pl.pallas_call(
        _ada_kernel,
        grid=(depth, n // tn),
        in_specs=[pl.BlockSpec((8, d), lambda l, j: (0, 0)),
                  pl.BlockSpec((1, d, tn), lambda l, j: (l, 0, j)),
                  pl.BlockSpec((1, 1, tn), lambda l, j: (l, 0, j))],
        out_specs=pl.BlockSpec((1, 8, tn), lambda l, j: (l, 0, j)),
        out_shape=jax.ShapeDtypeStruct((depth, 8, n), F32),
        compiler_params=_cparams(("arbitrary", "arbitrary"), 40),
        name="ada",
    )(c8, w_ada, b_ada.reshape(depth, 1, n))


def _proj_kernel(n_lat, x_ref, g_ref, sh_ref, sc_ref, w_ref, cos_ref, sin_ref, gq_ref, gkv_ref, wuq_ref, wukv_ref,
                 naq, nak, nav, dfq, dfqr, dfk, dfkr, dfv, mlqn, mlqp, mlqpr, mlkn, mlv, mlkx, s5u, gates):
    is_ctx = pl.program_id(1) >= n_lat
    m_lat = jnp.where(is_ctx, 0.0, 1.0).astype(F32)
    m_ctx = 1.0 - m_lat
    h = (_rms(x_ref[0], g_ref[...]) * (1.0 + sc_ref[0]) + sh_ref[0]).astype(BF16)

    def proj(off, width):
        return jnp.dot(h, w_ref[:, off:off + width], preferred_element_type=F32)

    cos = cos_ref[...]
    sin = sin_ref[...]
    a = proj(OFF_NA, 768)
    naq[0] = (a[:, :256] * NA_SCALE).astype(BF16)
    nak[0] = a[:, 256:512].astype(BF16)
    nav[0] = a[:, 512:].astype(BF16)

    bq = proj(OFF_DQ, 256) * DIFF_SCALE
    dfq[0] = bq.astype(BF16)
    dfqr[0] = _rope(bq, cos, sin).astype(BF16)
    bk = proj(OFF_DK, 256)
    dfk[0] = (bk * m_ctx).astype(BF16)
    dfkr[0] = (_rope(bk, cos, sin) * m_lat).astype(BF16)
    dfv[0] = proj(OFF_DV, 256).astype(BF16)

    cq = _rms(proj(OFF_CQ, 256), gq_ref[...]).astype(BF16)
    q2 = jnp.dot(cq, wuq_ref[...], preferred_element_type=F32) * MLA_SCALE
    qp = q2[:, 256:]
    mlqn[0] = q2[:, :256].astype(BF16)
    mlqp[0] = qp.astype(BF16)
    mlqpr[0] = _rope(qp, cos[:, :LANES], sin[:, :LANES]).astype(BF16)
    ckv = _rms(proj(OFF_CKV, 128), gkv_ref[...]).astype(BF16)
    kv = jnp.dot(ckv, wukv_ref[...], preferred_element_type=F32)
    mlkn[0] = kv[:, :256].astype(BF16)
    mlv[0] = kv[:, 256:].astype(BF16)
    kpe = proj(OFF_KPE, LANES)
    kper = _rope(kpe, cos[:, :LANES], sin[:, :LANES])
    mlkx[0] = (kper * m_lat + pltpu.roll(kpe, MLA_ROPE, 1) * m_ctx).astype(BF16)

    s5u[0] = proj(OFF_U, 256)
    d = x_ref.shape[-1]
    for i in range(N_BRANCH):
        gates[0, :, i * d:(i + 1) * d] = jax.nn.sigmoid(proj(OFF_GATE + i * d, d)).astype(BF16)


def _proj(x_all, g1, shift, scale, w_main, cos_t, sin_t, g_q, g_kv, w_uq, w_ukv, n_lat):
    b, t, d = x_all.shape
    nt = t // TM
    mod_spec = pl.BlockSpec((1, 1, d), lambda bi, i: (bi * 2 + (i >= n_lat).astype(I32), 0, 0))

    def full(a):
        return pl.BlockSpec(a.shape, lambda bi, i: (0,) * a.ndim)

    def tile(w):
        return pl.BlockSpec((1, TM, w), lambda bi, i: (bi, i, 0))

    widths = [256, 256, 256, 256, 256, 256, 256, 256, 256, 128, 128, 256, 256, 128, 256, N_BRANCH * d]
    dtypes = [BF16] * 14 + [F32, BF16]
    return pl.pallas_call(
        functools.partial(_proj_kernel, n_lat),
        grid=(b, nt),
        in_specs=[tile(d), full(g1), mod_spec, mod_spec, full(w_main),
                  pl.BlockSpec((TM, 256), lambda bi, i: (i, 0)), pl.BlockSpec((TM, 256), lambda bi, i: (i, 0)),
                  full(g_q), full(g_kv), full(w_uq), full(w_ukv)],
        out_specs=[tile(w) for w in widths],
        out_shape=[jax.ShapeDtypeStruct((b, t, w), dt) for w, dt in zip(widths, dtypes)],
        compiler_params=_cparams(("arbitrary", "arbitrary"), 56),
        name="proj",
    )(x_all, g1, shift, scale, w_main, cos_t, sin_t, g_q, g_kv, w_uq, w_ukv)


def _na_kernel(q_ref, k0, k1, k2, kc, v0, v1, v2, vc, bias_ref, o_ref):
    nt_dims = (((1,), (1,)), ((), ()))
    outs = []
    for h in range(NA_HEADS):
        sl = slice(h * HEAD_DIM, (h + 1) * HEAD_DIM)
        qh = q_ref[0, :, sl]
        sw = jnp.concatenate(
            [lax.dot_general(qh, kr[0, :, sl], nt_dims, preferred_element_type=F32) for kr in (k0, k1, k2)],
            axis=1) + bias_ref[0, h]
        sc = lax.dot_general(qh, kc[0, :, sl], nt_dims, preferred_element_type=F32)
        m = jnp.maximum(jnp.max(sw, axis=1, keepdims=True), jnp.max(sc, axis=1, keepdims=True))
        pw = jnp.exp(sw - m)
        pc = jnp.exp(sc - m)
        l = jnp.sum(pw, axis=1, keepdims=True) + jnp.sum(pc, axis=1, keepdims=True)
        o = jnp.dot(pc.astype(BF16), vc[0, :, sl], preferred_element_type=F32)
        for j, vr in enumerate((v0, v1, v2)):
            o = o + jnp.dot(pw[:, j * TM:(j + 1) * TM].astype(BF16), vr[0, :, sl], preferred_element_type=F32)
        outs.append(o / l)
    o_ref[0] = jnp.concatenate(outs, axis=1).astype(BF16)


def _na_bias(rpb):
    rows_q = TM // GRID_W
    rows_k = 3 * rows_q
    qr = np.arange(rows_q)
    kr = np.arange(rows_k)
    col = np.arange(GRID_W)
    cstart = np.clip(col - WIN_W // 2, 0, GRID_W - WIN_W)
    col_ok = (col[None, :] >= cstart[:, None]) & (col[None, :] < cstart[:, None] + WIN_W)
    dc = np.clip(col[None, :] - col[:, None], 1 - WIN_W, WIN_W - 1) + (WIN_W - 1)
    sel_c = (dc[:, :, None] == np.arange(2 * WIN_W - 1)).astype(np.float32)
    kinds = []
    for start_rel, q_rel in ((np.zeros_like(qr), qr), (qr, qr + 4), (np.full_like(qr, 4), qr + 8)):
        row_ok = (kr[None, :] >= start_rel[:, None]) & (kr[None, :] < start_rel[:, None] + WIN_H)
        dr = np.clip(kr[None, :] - q_rel[:, None] + (WIN_H - 1), 0, 2 * WIN_H - 2)
        sel_r = (dr[:, :, None] == np.arange(2 * WIN_H - 1)).astype(np.float32)
        bias = jnp.einsum('hab,rsa,cdb->hrcsd', rpb.astype(F32), sel_r, sel_c, precision=HIGHEST)
        ok = row_ok[:, None, :, None] & col_ok[None, :, None, :]
        kinds.append(jnp.where(ok[None], bias, NEG_INF).reshape(NA_HEADS, TM, 3 * TM))
    kinds.append(jnp.full((NA_HEADS, TM, 3 * TM), NEG_INF, F32))
    return jnp.stack(kinds)


def _na(q, k, v, bias, n_lat):
    b, t, w = q.shape
    nt = t // TM

    def base(i):
        return jnp.clip(i - 1, 0, n_lat - 3)

    def kind(i):
        return jnp.where(i >= n_lat, 3, jnp.where(i == 0, 0, jnp.where(i == n_lat - 1, 2, 1)))

    def win(j):
        return pl.BlockSpec((1, TM, w), lambda bi, i: (bi, base(i) + j, 0))

    ctx = pl.BlockSpec((1, TM, w), lambda bi, i: (bi, n_lat, 0))
    own = pl.BlockSpec((1, TM, w), lambda bi, i: (bi, i, 0))
    return pl.pallas_call(
        _na_kernel,
        grid=(b, nt),
        in_specs=[own, win(0), win(1), win(2), ctx, win(0), win(1), win(2), ctx,
                  pl.BlockSpec((1, NA_HEADS, TM, 3 * TM), lambda bi, i: (kind(i), 0, 0, 0))],
        out_specs=own,
        out_shape=jax.ShapeDtypeStruct((b, t, w), BF16),
        compiler_params=_cparams(("arbitrary", "arbitrary"), 40),
        name="na_attn",
    )(q, k, k, k, k, v, v, v, v, bias)


def _attend(n_maps, s_lat, tk, q_ref, kt_ref, v_ref, s_ref, emit):
    tq = q_ref.shape[2]
    t = v_ref.shape[2]
    dv = v_ref.shape[3] // 2
    dk = q_ref.shape[3] // n_maps
    is_ctx = pl.program_id(2) * tq >= s_lat
    maps = range(n_maps)
    qs = [q_ref[0, 0, :, m * dk:(m + 1) * dk] for m in maps]
    init = tuple((jnp.full((tq, 1), NEG_INF, F32), jnp.zeros((tq, 2 * dv), F32)) for _ in maps)

    def scores(slot, j):
        off = j * tk
        mx = []
        for m in maps:
            s = jnp.dot(qs[m], kt_ref[0, 0, m * dk:(m + 1) * dk, pl.ds(off, tk)], preferred_element_type=F32)
            s_ref[m, slot] = s
            mx.append(jnp.max(s, axis=1, keepdims=True))
        return tuple(mx)

    def update(s, mx, v1, carry):
        m_old, acc = carry
        m_new = jnp.maximum(m_old, mx)
        p = jnp.exp(s - m_new).astype(BF16)
        return m_new, jnp.exp(m_old - m_new) * acc + jnp.dot(p, v1, preferred_element_type=F32)

    def absorb(slot, j, mx, carry):
        v1 = v_ref[0, 0, pl.ds(j * tk, tk), :]
        return tuple(update(s_ref[m, slot], mx[m], v1, carry[m]) for m in maps)

    def ctx_scores():
        return [jnp.dot(qs[m], kt_ref[0, 0, m * dk:(m + 1) * dk, s_lat:t], preferred_element_type=F32) for m in maps]

    def finish(sc, carry):
        vc = v_ref[0, 0, s_lat:t, :]
        outs = []
        for m in maps:
            _, acc = update(sc[m], jnp.max(sc[m], axis=1, keepdims=True), vc, carry[m])
            outs.append(acc[:, :dv] / acc[:, dv:dv + 1])
        emit(outs)

    @pl.when(jnp.logical_not(is_ctx))
    def _():
        n_steps = s_lat // tk
        n_pairs = (n_steps - 1) // 2

        def pair(i, state):
            carry, mx = state
            j = 2 * i
            mx1 = scores(1, j + 1)
            carry = absorb(0, j, mx, carry)
            mx2 = scores(0, j + 2)
            return absorb(1, j + 1, mx1, carry), mx2

        state = (init, scores(0, 0))
        for i in range(n_pairs):
            state = pair(i, state)
        carry, mx = state
        j = 2 * n_pairs
        if n_steps - j == 2:
            mx1 = scores(1, j + 1)
            carry = absorb(0, j, mx, carry)
            sc = ctx_scores()
            carry = absorb(1, j + 1, mx1, carry)
        else:
            sc = ctx_scores()
            carry = absorb(0, j, mx, carry)
        finish(sc, carry)

    @pl.when(is_ctx)
    def _():
        finish(ctx_scores(), init)


def _mla_kernel(s_lat, tk, q_ref, kt_ref, v_ref, o_ref, s_ref):
    def emit(outs):
        o_ref[0, 0] = outs[0].astype(BF16)

    _attend(1, s_lat, tk, q_ref, kt_ref, v_ref, s_ref, emit)


def _diff_kernel(s_lat, tk, q_ref, kt_ref, v_ref, lam_ref, g_ref, o_ref, s_ref, *, lam_init):
    def emit(outs):
        lam = lam_ref[...]
        lam_full = (jnp.exp(jnp.sum(lam[0:1] * lam[1:2], axis=1, keepdims=True))
                    - jnp.exp(jnp.sum(lam[2:3] * lam[3:4], axis=1, keepdims=True)) + lam_init)
        d = _rms(outs[0] - lam_full * outs[1], g_ref[...]) * (1.0 - lam_init)
        o_ref[0, 0] = d.astype(BF16)

    _attend(2, s_lat, tk, q_ref, kt_ref, v_ref, s_ref, emit)


def _with_ones(v):
    pad = jnp.zeros(v.shape[:-1] + (v.shape[-1] - 1,), v.dtype)
    return jnp.concatenate([v, jnp.ones(v.shape[:-1] + (1,), v.dtype), pad], axis=-1)


def _global_attention(kernel, name, n_maps, q, kt, v, extra, s_lat):
    b, h, t, dq = q.shape
    dv = v.shape[-1]
    v = _with_ones(v)
    tk = min(1024, s_lat)
    in_specs = [pl.BlockSpec((1, 1, TM, dq), lambda bi, hi, i: (bi, hi, i, 0)),
                pl.BlockSpec((1, 1, kt.shape[2], t), lambda bi, hi, i: (bi, hi, 0, 0)),
                pl.BlockSpec((1, 1, t, 2 * dv), lambda bi, hi, i: (bi, hi, 0, 0))]
    in_specs += [pl.BlockSpec(a.shape, lambda bi, hi, i: (0, 0)) for a in extra]
    return pl.pallas_call(
        functools.partial(kernel, s_lat, tk),
        grid=(b, h, t // TM),
        in_specs=in_specs,
        out_specs=pl.BlockSpec((1, 1, TM, dv), lambda bi, hi, i: (bi, hi, i, 0)),
        out_shape=jax.ShapeDtypeStruct((b, h, t, dv), BF16),
        scratch_shapes=[pltpu.VMEM((n_maps, 2, TM, tk), F32)],
        compiler_params=_cparams(("arbitrary", "arbitrary", "arbitrary"), 48),
        name=name,
    )(q, kt, v, *extra)


def _s5_tables(lam_re, lam_im, log_step, b_re, b_im, c_re, c_im):
    c = S5_CHUNK
    lr, li = lam_re.astype(F32), lam_im.astype(F32)
    step = jnp.exp(log_step.astype(F32))[..., None]
    mag = jnp.exp(lr * step)
    ar, ai = mag * jnp.cos(li * step), mag * jnp.sin(li * step)
    den = lr * lr + li * li
    fr = ((ar - 1.0) * lr + ai * li) / den
    fi = (ai * lr - (ar - 1.0) * li) / den
    br, bi = b_re.astype(F32), b_im.astype(F32)
    bbr = fr[..., None] * br - fi[..., None] * bi
    bbi = fr[..., None] * bi + fi[..., None] * br
    tau = jnp.arange(c + 1, dtype=F32)[:, None, None, None]
    pmag = jnp.exp(lr * step * tau)
    pr, pi = pmag * jnp.cos(li * step * tau), pmag * jnp.sin(li * step * tau)
    cr, ci = c_re.astype(F32), c_im.astype(F32)
    abr = pr[..., None] * bbr - pi[..., None] * bbi
    abi = pr[..., None] * bbi + pi[..., None] * bbr
    kk = jnp.einsum('dgip,tdgpj->tdgij', cr, abr) - jnp.einsum('dgip,tdgpj->tdgij', ci, abi)
    s_idx = jnp.arange(c)[:, None]
    t_idx = jnp.arange(c)[None, :]
    lag_f = jnp.clip(t_idx - s_idx, 0, c)
    lag_b = jnp.clip(s_idx - t_idx, 0, c)
    tf = jnp.where((s_idx <= t_idx)[:, :, None, None, None], kk[lag_f, 0], 0.0)
    tb = jnp.where((s_idx >= t_idx)[:, :, None, None, None], kk[lag_b, 1], 0.0)
    g, j_ch = S5_GROUPS, S5_GROUP_CH
    tmat = (tf + tb).transpose(2, 0, 4, 1, 3).reshape(g, c * j_ch, c * j_ch)
    pow_f = jnp.arange(c - 1, -1, -1)
    pow_b = jnp.arange(c)

    def exit_map(d, pw):
        re = abr[pw, d].transpose(1, 0, 3, 2).reshape(g, c * j_ch, S5_STATE)
        im = abi[pw, d].transpose(1, 0, 3, 2).reshape(g, c * j_ch, S5_STATE)
        return jnp.concatenate([re, im], axis=-1)

    bsum = jnp.stack([exit_map(0, pow_f), exit_map(1, pow_b)])

    def entry_map(d, pw):
        p_r, p_i = pr[pw, d], pi[pw, d]
        on_re = jnp.einsum('gip,tgp->gpti', cr[d], p_r) - jnp.einsum('gip,tgp->gpti', ci[d], p_i)
        on_im = -jnp.einsum('gip,tgp->gpti', cr[d], p_i) - jnp.einsum('gip,tgp->gpti', ci[d], p_r)
        return jnp.concatenate([on_re, on_im], axis=1).reshape(g, 2 * S5_STATE, c * j_ch)

    cin = jnp.stack([entry_map(0, jnp.arange(1, c + 1)), entry_map(1, jnp.arange(c, 0, -1))])

    def pair_diag(a):
        a = a.reshape(a.shape[:-3] + (g // 2, 2) + a.shape[-2:])
        z = jnp.zeros_like(a[..., 0, :, :])
        return jnp.concatenate([jnp.concatenate([a[..., 0, :, :], z], axis=-1),
                                jnp.concatenate([z, a[..., 1, :, :]], axis=-1)], axis=-2).astype(BF16)

    a_c = tuple(a.reshape(2, 1, g * S5_STATE) for a in (pr[c], pi[c]))
    return (pair_diag(tmat), pair_diag(bsum[..., :S5_STATE]), pair_diag(bsum[..., S5_STATE:]),
            pair_diag(cin[..., :S5_STATE, :]), pair_diag(cin[..., S5_STATE:, :]), a_c)


def _s5_intra_kernel(u_ref, tm_ref, bre_ref, bim_ref, y_ref, sr_ref, si_ref):
    u = u_ref[0, 0]
    y_ref[0, 0] = jnp.dot(u, tm_ref[0], preferred_element_type=F32)
    for d in range(2):
        sr_ref[0, d] = jnp.dot(u, bre_ref[d, 0], preferred_element_type=F32)
        si_ref[0, d] = jnp.dot(u, bim_ref[d, 0], preferred_element_type=F32)


def _s5_intra(ug, tmat, bre, bim):
    b, gp, nc, w = ug.shape
    lanes = S5_GROUPS * S5_STATE
    summ = pl.BlockSpec((1, 2, nc, 2 * S5_STATE), lambda bi, pi: (bi, 0, 0, pi))
    return pl.pallas_call(
        _s5_intra_kernel,
        grid=(b, gp),
        in_specs=[pl.BlockSpec((1, 1, nc, w), lambda bi, pi: (bi, pi, 0, 0)),
                  pl.BlockSpec((1, w, w), lambda bi, pi: (pi, 0, 0)),
                  pl.BlockSpec((2, 1, w, 2 * S5_STATE), lambda bi, pi: (0, pi, 0, 0)),
                  pl.BlockSpec((2, 1, w, 2 * S5_STATE), lambda bi, pi: (0, pi, 0, 0))],
        out_specs=[pl.BlockSpec((1, 1, nc, w), lambda bi, pi: (bi, pi, 0, 0)), summ, summ],
        out_shape=[jax.ShapeDtypeStruct((b, gp, nc, w), F32),
                   jax.ShapeDtypeStruct((b, 2, nc, lanes), F32),
                   jax.ShapeDtypeStruct((b, 2, nc, lanes), F32)],
        compiler_params=_cparams(("arbitrary", "arbitrary")),
        name="s5_intra",
    )(ug, tmat, bre, bim)


def _s5_scan_kernel(sfr, sfi, sbr, sbi, ar_ref, ai_ref, hfr, hfi, hbr, hbi, st):
    @pl.when(pl.program_id(0) == 0)
    def _():
        st[...] = jnp.zeros_like(st)

    kb = sfr.shape[2]
    afr, afi, abr, abi = ar_ref[0], ai_ref[0], ar_ref[1], ai_ref[1]
    fr, fi, br, bi = st[0], st[1], st[2], st[3]
    for k in range(kb):
        up = slice(k, k + 1)
        dn = slice(kb - 1 - k, kb - k)
        hfr[:, up, :] = fr
        hfi[:, up, :] = fi
        hbr[:, dn, :] = br
        hbi[:, dn, :] = bi
        fr, fi = afr * fr - afi * fi + sfr[:, 0, up, :], afr * fi + afi * fr + sfi[:, 0, up, :]
        br, bi = abr * br - abi * bi + sbr[:, 0, dn, :], abr * bi + abi * br + sbi[:, 0, dn, :]
    st[0], st[1], st[2], st[3] = fr, fi, br, bi


def _s5_scan(sr, si, ar, ai, n_lat_blocks):
    b, _, nc, lanes = sr.shape
    kb = S5_CHUNK

    def f_blk(j):
        return jnp.where(j == 0, n_lat_blocks, j - 1)

    def b_blk(j):
        return jnp.where(j == 0, n_lat_blocks, n_lat_blocks - j)

    def summ(d, blk):
        return pl.BlockSpec((b, 1, kb, lanes), lambda j: (0, d, blk(j), 0))

    def state(blk):
        return pl.BlockSpec((b, kb, lanes), lambda j: (0, blk(j), 0))

    par = pl.BlockSpec((2, 1, lanes), lambda j: (0, 0, 0))
    return pl.pallas_call(
        _s5_scan_kernel,
        grid=(nc // kb,),
        in_specs=[summ(0, f_blk), summ(0, f_blk), summ(1, b_blk), summ(1, b_blk), par, par],
        out_specs=[state(f_blk), state(f_blk), state(b_blk), state(b_blk)],
        out_shape=[jax.ShapeDtypeStruct((b, nc, lanes), F32)] * 4,
        scratch_shapes=[pltpu.VMEM((4, b, 1, lanes), F32)],
        compiler_params=_cparams(("arbitrary",)),
        name="s5_scan",
    )(sr, si, sr, si, ar, ai)


def _s5_inter_kernel(y_ref, hfr, hfi, hbr, hbi, cre_ref, cim_ref, o_ref):
    y = y_ref[0, 0]
    for d, (hr, hi) in enumerate(((hfr, hfi), (hbr, hbi))):
        y = y + jnp.dot(hr[0].astype(BF16), cre_ref[d, 0], preferred_element_type=F32)
        y = y + jnp.dot(hi[0].astype(BF16), cim_ref[d, 0], preferred_element_type=F32)
    o_ref[0, 0] = y


def _s5_inter(y_intra, states, cre, cim):
    b, gp, nc, w = y_intra.shape
    own = pl.BlockSpec((1, 1, nc, w), lambda bi, pi: (bi, pi, 0, 0))
    st = pl.BlockSpec((1, nc, 2 * S5_STATE), lambda bi, pi: (bi, 0, pi))
    tab = pl.BlockSpec((2, 1, 2 * S5_STATE, w), lambda bi, pi: (0, pi, 0, 0))
    return pl.pallas_call(
        _s5_inter_kernel,
        grid=(b, gp),
        in_specs=[own, st, st, st, st, tab, tab],
        out_specs=own,
        out_shape=jax.ShapeDtypeStruct((b, gp, nc, w), F32),
        compiler_params=_cparams(("arbitrary", "arbitrary")),
        name="s5_inter",
    )(y_intra, *states, cre, cim)


def _s5_mixer(u, tables, s_lat):
    tmat, bre, bim, cre, cim, (acr, aci) = tables
    b, t, w = u.shape
    c, gp, jc = S5_CHUNK, S5_GROUPS // 2, S5_GROUP_CH
    nc = t // c
    assert (t - s_lat) == c * c and s_lat % (c * c) == 0
    ug = u.reshape(b, nc, c, gp, 2, jc).transpose(0, 3, 1, 4, 2, 5).reshape(b, gp, nc, 2 * c * jc).astype(BF16)
    y_intra, sr, si = _s5_intra(ug, tmat, bre, bim)
    states = _s5_scan(sr, si, acr, aci, s_lat // (c * c))
    y = _s5_inter(y_intra, states, cre, cim)
    return y.reshape(b, gp, nc, 2, c, jc).transpose(0, 2, 4, 1, 3, 5).reshape(b, t, w)


def _merge_kernel(ya, yb, ym, ys, su, dd, wglu, bglu, gates, wb, wo, x_ref, gt1, g2, sh2, sc2, wr, eb,
                  xo_ref, hp_ref, idx_ref, gw_ref, hist_ref):
    d = x_ref.shape[-1]
    y5 = ys[0] + dd[...] * su[0]
    gl = jax.nn.gelu(y5)
    yd = gl * jax.nn.sigmoid(jnp.dot(gl, wglu[...], preferred_element_type=F32) + bglu[...])
    branches = (ya[0], yb[0], ym[0], yd.astype(BF16))
    m = None
    for i in range(N_BRANCH):
        term = gates[0, :, i * d:(i + 1) * d].astype(F32) * jnp.dot(branches[i], wb[i], preferred_element_type=F32)
        m = term if m is None else m + term
    xn = x_ref[0] + gt1[0] * jnp.dot(m.astype(BF16), wo[...], preferred_element_type=F32)
    xo_ref[0] = xn
    h2 = _rms(xn, g2[...]) * (1.0 + sc2[0]) + sh2[0]
    hp_ref[0] = _pack_bf16_pair(h2)

    scores = jax.nn.sigmoid(jnp.dot(h2, wr[...], preferred_element_type=F32, precision=HIGHEST))
    sel = scores + eb[...]
    lane = lax.broadcasted_iota(I32, sel.shape, 1).astype(F32)
    idx_acc = jnp.zeros(sel.shape, F32)
    gw_acc = jnp.zeros(sel.shape, F32)
    chosen = jnp.zeros(sel.shape, F32)
    for k in range(TOP_K):
        mx = jnp.max(sel, axis=1, keepdims=True)
        ik = jnp.min(jnp.where(sel == mx, lane, float(LANES)), axis=1, keepdims=True)
        hit = lane == ik
        gk = jnp.sum(jnp.where(hit, scores, 0.0), axis=1, keepdims=True)
        idx_acc = jnp.where(lane == k, ik, idx_acc)
        gw_acc = jnp.where(lane == k, gk, gw_acc)
        chosen = jnp.where(hit, 1.0, chosen)
        sel = jnp.where(hit, -jnp.inf, sel)
    idx_ref[0] = idx_acc.astype(I32)
    gw_ref[0] = gw_acc / jnp.sum(gw_acc, axis=1, keepdims=True) * ROUTED_SCALE
    hist_ref[0, 0] = jnp.sum(chosen, axis=0, keepdims=True)


def _merge(ya, yb, ym, ys, su, dd, wglu, bglu, gates, wb, wo, x_all, gt1, g2, sh2, sc2, wr, eb, n_lat):
    b, t, d = x_all.shape
    mod_spec = pl.BlockSpec((1, 1, d), lambda bi, i: (bi * 2 + (i >= n_lat).astype(I32), 0, 0))

    def full(a):
        return pl.BlockSpec(a.shape, lambda bi, i: (0,) * a.ndim)

    def tile(w):
        return pl.BlockSpec((1, TM, w), lambda bi, i: (bi, i, 0))

    return pl.pallas_call(
        _merge_kernel,
        grid=(b, t // TM),
        in_specs=[tile(256), tile(256), tile(256), tile(256), tile(256), full(dd), full(wglu), full(bglu),
                  tile(N_BRANCH * d), full(wb), full(wo), tile(d), mod_spec, full(g2), mod_spec, mod_spec,
                  full(wr), full(eb)],
        out_specs=[tile(d), tile(d // 2), tile(LANES), tile(LANES),
                   pl.BlockSpec((1, 1, 1, LANES), lambda bi, i: (bi, i, 0, 0))],
        out_shape=[jax.ShapeDtypeStruct((b, t, d), F32), jax.ShapeDtypeStruct((b, t, d // 2), I32),
                   jax.ShapeDtypeStruct((b, t, LANES), I32), jax.ShapeDtypeStruct((b, t, LANES), F32),
                   jax.ShapeDtypeStruct((b, t // TM, 1, LANES), F32)],
        compiler_params=_cparams(("arbitrary", "arbitrary"), 48),
        name="merge",
    )(ya, yb, ym, ys, su, dd, wglu, bglu, gates, wb, wo, x_all, gt1, g2, sh2, sc2, wr, eb)


def _route_tables(hist, n_assign):
    h = hist.reshape(-1, LANES).astype(I32)
    counts = jnp.sum(h, axis=0)
    padded = (counts + EXPERT_BLOCK - 1) // EXPERT_BLOCK * EXPERT_BLOCK
    pends = jnp.cumsum(padded)
    base = (pends - padded)[None, :] + jnp.cumsum(h, axis=0) - h
    n_blocks = -(-n_assign // EXPERT_BLOCK) + N_EXPERTS
    block_start = jnp.arange(n_blocks, dtype=I32) * EXPERT_BLOCK
    blk_e = jnp.minimum(jnp.sum((pends[None, :N_EXPERTS] <= block_start[:, None]).astype(I32), axis=1),
                        N_EXPERTS - 1)
    return base.astype(F32).reshape(hist.shape), blk_e, n_blocks


def _pos_kernel(idx_ref, base_ref, pos_ref):
    idx = idx_ref[0]
    lane = lax.broadcasted_iota(I32, idx.shape, 1)
    hits = [lane == idx[:, k:k + 1] for k in range(TOP_K)]
    chosen = jnp.zeros(idx.shape, F32)
    for hit in hits:
        chosen = jnp.where(hit, 1.0, chosen)
    row = lax.broadcasted_iota(I32, (TM, TM), 0)
    col = lax.broadcasted_iota(I32, (TM, TM), 1)
    earlier = jnp.where(row > col, 1.0, 0.0).astype(BF16)
    slot = jnp.dot(earlier, chosen.astype(BF16), preferred_element_type=F32) + base_ref[0, 0]
    pos = jnp.zeros(idx.shape, F32)
    for k, hit in enumerate(hits):
        pos = jnp.where(lane == k, jnp.sum(jnp.where(hit, slot, 0.0), axis=1, keepdims=True), pos)
    pos_ref[0] = pos.astype(I32)


def _positions(idx, base):
    b, t, _ = idx.shape
    tile = pl.BlockSpec((1, TM, LANES), lambda bi, i: (bi, i, 0))
    return pl.pallas_call(
        _pos_kernel,
        grid=(b, t // TM),
        in_specs=[tile, pl.BlockSpec((1, 1, 1, LANES), lambda bi, i: (bi, i, 0, 0))],
        out_specs=tile,
        out_shape=jax.ShapeDtypeStruct((b, t, LANES), I32),
        compiler_params=_cparams(("arbitrary", "arbitrary")),
        name="moe_pos",
    )(idx, base)


def _scatter_kernel(pos_ref, hp_ref, init_ref, xs_ref, sem):
    del init_ref
    for j in range(TOP_K * TM):
        pltpu.make_async_copy(hp_ref.at[pl.ds(j % TM, 1)], xs_ref.at[pl.ds(pos_ref[0, 0, j], 1)], sem).start()
    for k in range(TOP_K):
        pltpu.make_async_copy(hp_ref, xs_ref.at[pl.ds(0, TM)], sem).wait()


def _tile_major(pos):
    nt = pos.shape[0] // TM
    return pos[:, :TOP_K].reshape(nt, TM, TOP_K).transpose(0, 2, 1).reshape(nt, 1, TOP_K * TM)


def _scatter_rows(pos3, hp, n_slots):
    n, half = hp.shape
    return pl.pallas_call(
        _scatter_kernel,
        grid=(n // TM,),
        in_specs=[pl.BlockSpec((1, 1, TOP_K * TM), lambda i: (i, 0, 0), memory_space=pltpu.SMEM),
                  pl.BlockSpec((TM, half), lambda i: (i, 0)),
                  pl.BlockSpec(memory_space=pl.ANY)],
        out_specs=pl.BlockSpec(memory_space=pl.ANY),
        out_shape=jax.ShapeDtypeStruct((n_slots, half), hp.dtype),
        scratch_shapes=[pltpu.SemaphoreType.DMA(())],
        input_output_aliases={2: 0},
        compiler_params=_cparams(("arbitrary",)),
        name="moe_scatter",
    )(pos3, hp, jnp.zeros((n_slots, half), hp.dtype))


def _swiglu_packed(p, w1, w3, w2):
    lo, hi = _unpack_bf16_pair(p)
    lo, hi = lo.astype(BF16), hi.astype(BF16)
    n = lo.shape[-1]

    def up(w):
        return (jnp.dot(lo, w[:n], preferred_element_type=F32) + jnp.dot(hi, w[n:], preferred_element_type=F32))

    a = up(w1)
    return jnp.dot((a * jax.nn.sigmoid(a) * up(w3)).astype(BF16), w2, preferred_element_type=F32)


def _ffn_kernel(be_ref, x_ref, w1_ref, w3_ref, w2_ref, y_ref):
    del be_ref
    y_ref[...] = _pack_bf16_pair(_swiglu_packed(x_ref[...], w1_ref[0], w3_ref[0], w2_ref[0]))


def _expert_ffn(blk_e, xs, w1, w3, w2):
    n_slots, half = xs.shape
    e, d, ff = w1.shape
    row = pl.BlockSpec((EXPERT_BLOCK, half), lambda i, be: (i, 0))
    return pl.pallas_call(
        _ffn_kernel,
        grid_spec=pltpu.PrefetchScalarGridSpec(
            num_scalar_prefetch=1,
            grid=(n_slots // EXPERT_BLOCK,),
            in_specs=[row,
                      pl.BlockSpec((1, d, ff), lambda i, be: (be[i], 0, 0)),
                      pl.BlockSpec((1, d, ff), lambda i, be: (be[i], 0, 0)),
                      pl.BlockSpec((1, ff, d), lambda i, be: (be[i], 0, 0))],
            out_specs=row),
        out_shape=jax.ShapeDtypeStruct((n_slots, half), I32),
        compiler_params=_cparams(("arbitrary",)),
        name="moe_ffn",
    )(blk_e, xs, w1, w3, w2)


def _combine_kernel(pos_ref, gw_ref, ys_ref, hp_ref, x_ref, gt2, ws1, ws3, ws2, xo_ref, buf, sem):
    for j in range(TOP_K * TM):
        pltpu.make_async_copy(ys_ref.at[pl.ds(pos_ref[0, 0, j], 1)], buf.at[j // TM, pl.ds(j % TM, 1)], sem).start()

    shared = _swiglu_packed(hp_ref[...], ws1[...], ws3[...], ws2[...])

    for k in range(TOP_K):
        pltpu.make_async_copy(ys_ref.at[pl.ds(0, TM)], buf.at[k], sem).wait()

    gw = gw_ref[...]
    lo = hi = None
    for k in range(TOP_K):
        wk = gw[:, k:k + 1]
        lk, hk = _unpack_bf16_pair(buf[k])
        lo = wk * lk if lo is None else lo + wk * lk
        hi = wk * hk if hi is None else hi + wk * hk
    xo_ref[...] = x_ref[...] + gt2[0] * (shared + jnp.concatenate([lo, hi], axis=1))


def _combine(pos3, gw, ys, hp, x_flat, gt2, ws1, ws3, ws2, tiles_per_batch, n_lat):
    n, d = x_flat.shape
    nt = n // TM

    def full(a):
        return pl.BlockSpec(a.shape, lambda i: (0,) * a.ndim)

    def tile(w):
        return pl.BlockSpec((TM, w), lambda i: (i, 0))

    def mod_idx(i):
        return (i // tiles_per_batch) * 2 + ((i % tiles_per_batch) >= n_lat).astype(I32)

    return pl.pallas_call(
        _combine_kernel,
        grid=(nt,),
        in_specs=[pl.BlockSpec((1, 1, TOP_K * TM), lambda i: (i, 0, 0), memory_space=pltpu.SMEM),
                  tile(LANES), pl.BlockSpec(memory_space=pl.ANY), tile(d // 2), tile(d),
                  pl.BlockSpec((1, 1, d), lambda i: (mod_idx(i), 0, 0)), full(ws1), full(ws3), full(ws2)],
        out_specs=tile(d),
        out_shape=jax.ShapeDtypeStruct((n, d), F32),
        scratch_shapes=[pltpu.VMEM((TOP_K, TM, d // 2), I32), pltpu.SemaphoreType.DMA(())],
        compiler_params=_cparams(("arbitrary",), 40),
        name="moe_combine",
    )(pos3, gw, ys, hp, x_flat, gt2, ws1, ws3, ws2)


def _final_kernel(x_ref, g_ref, o_ref):
    o_ref[0] = _rms(x_ref[0], g_ref[...])


def _final_norm(x_all, g, s_lat):
    b, _, d = x_all.shape
    return pl.pallas_call(
        _final_kernel,
        grid=(b, s_lat // TM),
        in_specs=[pl.BlockSpec((1, TM, d), lambda bi, i: (bi, i, 0)), pl.BlockSpec((1, d), lambda bi, i: (0, 0))],
        out_specs=pl.BlockSpec((1, TM, d), lambda bi, i: (bi, i, 0)),
        out_shape=jax.ShapeDtypeStruct((b, s_lat, d), F32),
        compiler_params=_cparams(("arbitrary", "arbitrary")),
        name="final_norm",
    )(x_all, g.reshape(1, d))


def _rope_tables(s_lat, t):
    pos = jnp.arange(s_lat, dtype=I32)
    half = 8
    freqs = ROPE_BASE ** (-jnp.arange(half, dtype=F32) * 2.0 / (2 * half))

    def part(p):
        ang = p.astype(F32)[:, None] * freqs[None, :]
        c, s = jnp.cos(ang), jnp.sin(ang)
        return jnp.concatenate([c, c], axis=1), jnp.concatenate([-s, s], axis=1)

    cr, sr = part(pos // GRID_W)
    cc, sc = part(pos % GRID_W)
    cos = jnp.tile(jnp.concatenate([cr, cc], axis=1), (1, 8))
    sin = jnp.tile(jnp.concatenate([sr, sc], axis=1), (1, 8))
    pad = t - s_lat
    return (jnp.concatenate([cos, jnp.ones((pad, 256), F32)], axis=0),
            jnp.concatenate([sin, jnp.zeros((pad, 256), F32)], axis=0))


def _heads(x, h):
    b, t, w = x.shape
    return x.reshape(b, t, h, w // h).transpose(0, 2, 1, 3)


def _tokens(o):
    b, h, t, d = o.shape
    return o.transpose(0, 2, 1, 3).reshape(b, t, h * d)


def kernel(x, c, ctx, c_ctx, w_ada, b_ada, g_norm1, g_norm2, w_in, na_rpb, diff_lambda, g_diff, g_mla_q, g_mla_kv, w_mla_uq, w_mla_ukv, s5_lam_re, s5_lam_im, s5_log_step, s5_b_re, s5_b_im, s5_c_re, s5_c_im, s5_d, w_glu, b_glu, w_branch, w_out, w_router, e_bias, w_e1, w_e3, w_e2, w_s1, w_s3, w_s2, g_final):
    b, s_lat, d = x.shape
    lc = ctx.shape[1]
    t = s_lat + lc
    depth = w_ada.shape[0]
    n_lat = s_lat // TM
    assert lc == TM and s_lat % TM == 0 and n_lat >= 3 and b + 1 <= 8 and d % 256 == 0
    tiles_per_batch = t // TM

    x_all = jnp.concatenate([x, ctx], axis=1)
    c8 = jnp.zeros((8, d), F32).at[:b].set(c).at[b].set(c_ctx)
    ada = _ada(c8, w_ada, b_ada)
    cos_t, sin_t = _rope_tables(s_lat, t)

    hq = jnp.arange(MLA_HEADS)[:, None] * (MLA_NOPE + MLA_ROPE)
    uq_cols = jnp.concatenate([(hq + jnp.arange(MLA_NOPE)[None]).reshape(-1),
                               (hq + MLA_NOPE + jnp.arange(MLA_ROPE)[None]).reshape(-1)])
    hk = jnp.arange(MLA_HEADS)[:, None] * (MLA_NOPE + MLA_V)
    ukv_cols = jnp.concatenate([(hk + jnp.arange(MLA_NOPE)[None]).reshape(-1),
                                (hk + MLA_NOPE + jnp.arange(MLA_V)[None]).reshape(-1)])
    kpe_end = 1952

    for l in range(depth):
        mods = jnp.concatenate([ada[l, :b, None, :], jnp.broadcast_to(ada[l, b], (b, 1, 6 * d))], axis=1)
        sh1, sc1, gt1, sh2, sc2, gt2 = (mods[:, :, i * d:(i + 1) * d].reshape(2 * b, 1, d) for i in range(6))
        w_main = jnp.concatenate([w_in[l][:, :kpe_end], jnp.zeros((d, OFF_U - kpe_end), F32),
                                  w_in[l][:, kpe_end:]], axis=1).astype(BF16)
        (naq, nak, nav, dfq, dfqr, dfk, dfkr, dfv, mlqn, mlqp, mlqpr, mlkn, mlv, mlkx, s5u, gates) = _proj(
            x_all, g_norm1[l].reshape(1, d), sh1, sc1, w_main, cos_t, sin_t, g_mla_q[l].reshape(1, -1),
            g_mla_kv[l].reshape(1, -1), w_mla_uq[l][:, uq_cols].astype(BF16), w_mla_ukv[l][:, ukv_cols].astype(BF16),
            n_lat)

        ya = _na(naq, nak, nav, _na_bias(na_rpb[l]), n_lat)

        lam_init = 0.8 - 0.6 * math.exp(-0.3 * l)
        dq = jnp.concatenate([_heads(dfqr, 2 * DIFF_HEADS), _heads(dfq, 2 * DIFF_HEADS)], axis=-1)
        dq = dq.reshape(b, DIFF_HEADS, 2, t, 2 * DIFF_DK).transpose(0, 1, 3, 2, 4).reshape(b, DIFF_HEADS, t, -1)
        dk = jnp.concatenate([_heads(dfkr, 2 * DIFF_HEADS), _heads(dfk, 2 * DIFF_HEADS)], axis=-1)
        dkt = dk.transpose(0, 1, 3, 2).reshape(b, DIFF_HEADS, 4 * DIFF_DK, t)
        yb = _tokens(_global_attention(
            functools.partial(_diff_kernel, lam_init=lam_init), "diff_attn", 2, dq, dkt, _heads(dfv, DIFF_HEADS),
            (diff_lambda[l], g_diff[l].reshape(1, -1)), s_lat))

        mq = jnp.concatenate([_heads(mlqn, MLA_HEADS), _heads(mlqpr, MLA_HEADS), _heads(mlqp, MLA_HEADS)], axis=-1)
        kx = jnp.broadcast_to(mlkx[:, None, :, :2 * MLA_ROPE], (b, MLA_HEADS, t, 2 * MLA_ROPE))
        mkt = jnp.concatenate([_heads(mlkn, MLA_HEADS), kx], axis=-1).transpose(0, 1, 3, 2)
        ym = _tokens(_global_attention(_mla_kernel, "mla_attn", 1, mq, mkt, _heads(mlv, MLA_HEADS), (), s_lat))

        ys = _s5_mixer(s5u, _s5_tables(s5_lam_re[l], s5_lam_im[l], s5_log_step[l], s5_b_re[l], s5_b_im[l],
                                       s5_c_re[l], s5_c_im[l]), s_lat)

        wr = jnp.concatenate([w_router[l], jnp.zeros((d, LANES - N_EXPERTS), F32)], axis=1)
        eb = jnp.concatenate([e_bias[l].astype(F32), jnp.full((LANES - N_EXPERTS,), -jnp.inf, F32)]).reshape(1, LANES)
        x_all, hp, idx, gw, hist = _merge(
            ya, yb, ym, ys, s5u, s5_d[l].reshape(1, -1), w_glu[l], b_glu[l].reshape(1, -1), gates,
            w_branch[l].astype(BF16), w_out[l].astype(BF16), x_all, gt1, g_norm2[l].reshape(1, d), sh2, sc2, wr, eb,
            n_lat)

        n = b * t
        base, blk_e, n_blocks = _route_tables(hist, n * TOP_K)
        pos3 = _tile_major(_positions(idx, base).reshape(n, LANES))
        hp_flat = hp.reshape(n, d // 2)
        xs = _scatter_rows(pos3, hp_flat, n_blocks * EXPERT_BLOCK)
        y_sorted = _expert_ffn(blk_e, xs, w_e1[l].astype(BF16), w_e3[l].astype(BF16), w_e2[l].astype(BF16))
        x_all = _combine(pos3, gw.reshape(n, LANES), y_sorted, hp_flat, x_all.reshape(n, d), gt2,
                         w_s1[l].astype(BF16), w_s3[l].astype(BF16), w_s2[l].astype(BF16),
                         tiles_per_batch, n_lat).reshape(b, t, d)

    return _final_norm(x_all, g_final, s_lat)
```

```python
import functools
import math

import jax
import jax.numpy as jnp
import numpy as np
from jax import lax
from jax.experimental import pallas as pl
from jax.experimental.pallas import tpu as pltpu

F32 = jnp.float32
BF16 = jnp.bfloat16
I32 = jnp.int32
HIGHEST = lax.Precision.HIGHEST

GRID_W = 64
HEAD_DIM = 64
NA_HEADS = 4
WIN_H = 8
WIN_W = 16
DIFF_HEADS = 4
DIFF_DK = 32
DIFF_DV = 64
MLA_HEADS = 4
MLA_NOPE = 64
MLA_ROPE = 32
MLA_V = 64
S5_GROUPS = 16
S5_GROUP_CH = 16
S5_STATE = 64
S5_WIDTH = S5_GROUPS * S5_GROUP_CH
S5_CHUNK = 16
N_BRANCH = 4
N_EXPERTS = 64
TOP_K = 6
EXPERT_BLOCK = 256
ROUTED_SCALE = 1.0
ROPE_BASE = 10000.0
EPS = 1e-6
NEG_INF = -1e30

TM = 256
ATTN_TQ = 512
LANES = 128
HI_MASK = -65536

NA_SCALE = HEAD_DIM ** -0.5
DIFF_SCALE = DIFF_DK ** -0.5
MLA_SCALE = (MLA_NOPE + MLA_ROPE) ** -0.5

OFF_NA, OFF_DQ, OFF_DK, OFF_DV, OFF_CQ, OFF_CKV, OFF_KPE, OFF_U, OFF_GATE = (
    0, 768, 1024, 1280, 1536, 1792, 1920, 2048, 2304)


def _cparams(sem, vmem_mb=None):
    kw = dict(dimension_semantics=sem)
    if vmem_mb is not None:
        kw["vmem_limit_bytes"] = vmem_mb * 2 ** 20
    return pltpu.CompilerParams(**kw)


def _rms(x, g):
    return x * lax.rsqrt(jnp.mean(x * x, axis=-1, keepdims=True) + EPS) * g


def _rope(x, cos, sin_signed):
    w = x.shape[-1]
    lane = lax.broadcasted_iota(I32, x.shape, 1)
    partner = jnp.where((lane & 8) == 0, pltpu.roll(x, w - 8, 1), pltpu.roll(x, 8, 1))
    return x * cos + partner * sin_signed


def _pack_bf16_pair(x):
    n = x.shape[-1] // 2
    bits = lax.bitcast_convert_type(x.astype(BF16).astype(F32), I32)
    return lax.shift_right_logical(bits[:, :n], 16) | (bits[:, n:] & HI_MASK)


def _unpack_bf16_pair(p):
    lo = lax.bitcast_convert_type(lax.shift_left(p, 16), F32)
    hi = lax.bitcast_convert_type(p & HI_MASK, F32)
    return lo, hi


def _ada_kernel(c_ref, w_ref, b_ref, o_ref):
    c = c_ref[...]
    s = c * jax.nn.sigmoid(c)
    o_ref[0] = jnp.dot(s, w_ref[0], preferred_element_type=F32, precision=HIGHEST) + b_ref[0]


def _ada(c8, w_ada, b_ada):
    depth, d, n = w_ada.shape
    tn = 1536
    return pl.pallas_call(
        _ada_kernel,
        grid=(depth, n // tn),
        in_specs=[pl.BlockSpec((8, d), lambda l, j: (0, 0)),
                  pl.BlockSpec((1, d, tn), lambda l, j: (l, 0, j)),
                  pl.BlockSpec((1, 1, tn), lambda l, j: (l, 0, j))],
        out_specs=pl.BlockSpec((1, 8, tn), lambda l, j: (l, 0, j)),
        out_shape=jax.ShapeDtypeStruct((depth, 8, n), F32),
        compiler_params=_cparams(("arbitrary", "arbitrary"), 40),
        name="ada",
    )(c8, w_ada, b_ada.reshape(depth, 1, n))


def _proj_kernel(n_lat, x_ref, g_ref, sh_ref, sc_ref, w_ref, cos_ref, sin_ref, gq_ref, gkv_ref, wuq_ref, wukv_ref,
                 naq, nak, nav, dfq, dfqr, dfk, dfkr, dfv, mlqn, mlqp, mlqpr, mlkn, mlv, mlkx, s5u, gates):
    is_ctx = pl.program_id(1) >= n_lat
    m_lat = jnp.where(is_ctx, 0.0, 1.0).astype(F32)
    m_ctx = 1.0 - m_lat
    h = (_rms(x_ref[0], g_ref[...]) * (1.0 + sc_ref[0]) + sh_ref[0]).astype(BF16)

    def proj(off, width):
        return jnp.dot(h, w_ref[:, off:off + width], preferred_element_type=F32)

    cos = cos_ref[...]
    sin = sin_ref[...]
    a = proj(OFF_NA, 768)
    naq[0] = (a[:, :256] * NA_SCALE).astype(BF16)
    nak[0] = a[:, 256:512].astype(BF16)
    nav[0] = a[:, 512:].astype(BF16)

    bq = proj(OFF_DQ, 256) * DIFF_SCALE
    dfq[0] = bq.astype(BF16)
    dfqr[0] = _rope(bq, cos, sin).astype(BF16)
    bk = proj(OFF_DK, 256)
    dfk[0] = (bk * m_ctx).astype(BF16)
    dfkr[0] = (_rope(bk, cos, sin) * m_lat).astype(BF16)
    dfv[0] = proj(OFF_DV, 256).astype(BF16)

    cq = _rms(proj(OFF_CQ, 256), gq_ref[...]).astype(BF16)
    q2 = jnp.dot(cq, wuq_ref[...], preferred_element_type=F32) * MLA_SCALE
    qp = q2[:, 256:]
    mlqn[0] = q2[:, :256].astype(BF16)
    mlqp[0] = qp.astype(BF16)
    mlqpr[0] = _rope(qp, cos[:, :LANES], sin[:, :LANES]).astype(BF16)
    ckv = _rms(proj(OFF_CKV, 128), gkv_ref[...]).astype(BF16)
    kv = jnp.dot(ckv, wukv_ref[...], preferred_element_type=F32)
    mlkn[0] = kv[:, :256].astype(BF16)
    mlv[0] = kv[:, 256:].astype(BF16)
    kpe = proj(OFF_KPE, LANES)
    kper = _rope(kpe, cos[:, :LANES], sin[:, :LANES])
    mlkx[0] = (kper * m_lat + pltpu.roll(kpe, MLA_ROPE, 1) * m_ctx).astype(BF16)

    s5u[0] = proj(OFF_U, 256)
    d = x_ref.shape[-1]
    for i in range(N_BRANCH):
        gates[0, :, i * d:(i + 1) * d] = jax.nn.sigmoid(proj(OFF_GATE + i * d, d)).astype(BF16)


def _proj(x_all, g1, shift, scale, w_main, cos_t, sin_t, g_q, g_kv, w_uq, w_ukv, n_lat):
    b, t, d = x_all.shape
    nt = t // TM
    mod_spec = pl.BlockSpec((1, 1, d), lambda bi, i: (bi * 2 + (i >= n_lat).astype(I32), 0, 0))

    def full(a):
        return pl.BlockSpec(a.shape, lambda bi, i: (0,) * a.ndim)

    def tile(w):
        return pl.BlockSpec((1, TM, w), lambda bi, i: (bi, i, 0))

    widths = [256, 256, 256, 256, 256, 256, 256, 256, 256, 128, 128, 256, 256, 128, 256, N_BRANCH * d]
    dtypes = [BF16] * 14 + [F32, BF16]
    return pl.pallas_call(
        functools.partial(_proj_kernel, n_lat),
        grid=(b, nt),
        in_specs=[tile(d), full(g1), mod_spec, mod_spec, full(w_main),
                  pl.BlockSpec((TM, 256), lambda bi, i: (i, 0)), pl.BlockSpec((TM, 256), lambda bi, i: (i, 0)),
                  full(g_q), full(g_kv), full(w_uq), full(w_ukv)],
        out_specs=[tile(w) for w in widths],
        out_shape=[jax.ShapeDtypeStruct((b, t, w), dt) for w, dt in zip(widths, dtypes)],
        compiler_params=_cparams(("arbitrary", "arbitrary"), 56),
        name="proj",
    )(x_all, g1, shift, scale, w_main, cos_t, sin_t, g_q, g_kv, w_uq, w_ukv)


def _na_kernel(q_ref, k0, k1, k2, kc, v0, v1, v2, vc, bias_ref, o_ref):
    nt_dims = (((1,), (1,)), ((), ()))
    outs = []
    for h in range(NA_HEADS):
        sl = slice(h * HEAD_DIM, (h + 1) * HEAD_DIM)
        qh = q_ref[0, :, sl]
        sw = jnp.concatenate(
            [lax.dot_general(qh, kr[0, :, sl], nt_dims, preferred_element_type=F32) for kr in (k0, k1, k2)],
            axis=1) + bias_ref[0, h]
        sc = lax.dot_general(qh, kc[0, :, sl], nt_dims, preferred_element_type=F32)
        m = jnp.maximum(jnp.max(sw, axis=1, keepdims=True), jnp.max(sc, axis=1, keepdims=True))
        pw = jnp.exp(sw - m)
        pc = jnp.exp(sc - m)
        l = jnp.sum(pw, axis=1, keepdims=True) + jnp.sum(pc, axis=1, keepdims=True)
        o = jnp.dot(pc.astype(BF16), vc[0, :, sl], preferred_element_type=F32)
        for j, vr in enumerate((v0, v1, v2)):
            o = o + jnp.dot(pw[:, j * TM:(j + 1) * TM].astype(BF16), vr[0, :, sl], preferred_element_type=F32)
        outs.append(o / l)
    o_ref[0] = jnp.concatenate(outs, axis=1).astype(BF16)


def _na_bias(rpb):
    rows_q = TM // GRID_W
    rows_k = 3 * rows_q
    qr = np.arange(rows_q)
    kr = np.arange(rows_k)
    col = np.arange(GRID_W)
    cstart = np.clip(col - WIN_W // 2, 0, GRID_W - WIN_W)
    col_ok = (col[None, :] >= cstart[:, None]) & (col[None, :] < cstart[:, None] + WIN_W)
    dc = np.clip(col[None, :] - col[:, None], 1 - WIN_W, WIN_W - 1) + (WIN_W - 1)
    sel_c = (dc[:, :, None] == np.arange(2 * WIN_W - 1)).astype(np.float32)
    kinds = []
    for start_rel, q_rel in ((np.zeros_like(qr), qr), (qr, qr + 4), (np.full_like(qr, 4), qr + 8)):
        row_ok = (kr[None, :] >= start_rel[:, None]) & (kr[None, :] < start_rel[:, None] + WIN_H)
        dr = np.clip(kr[None, :] - q_rel[:, None] + (WIN_H - 1), 0, 2 * WIN_H - 2)
        sel_r = (dr[:, :, None] == np.arange(2 * WIN_H - 1)).astype(np.float32)
        bias = jnp.einsum('hab,rsa,cdb->hrcsd', rpb.astype(F32), sel_r, sel_c, precision=HIGHEST)
        ok = row_ok[:, None, :, None] & col_ok[None, :, None, :]
        kinds.append(jnp.where(ok[None], bias, NEG_INF).reshape(NA_HEADS, TM, 3 * TM))
    kinds.append(jnp.full((NA_HEADS, TM, 3 * TM), NEG_INF, F32))
    return jnp.stack(kinds)


def _na(q, k, v, bias, n_lat):
    b, t, w = q.shape
    nt = t // TM

    def base(i):
        return jnp.clip(i - 1, 0, n_lat - 3)

    def kind(i):
        return jnp.where(i >= n_lat, 3, jnp.where(i == 0, 0, jnp.where(i == n_lat - 1, 2, 1)))

    def win(j):
        return pl.BlockSpec((1, TM, w), lambda bi, i: (bi, base(i) + j, 0))

    ctx = pl.BlockSpec((1, TM, w), lambda bi, i: (bi, n_lat, 0))
    own = pl.BlockSpec((1, TM, w), lambda bi, i: (bi, i, 0))
    return pl.pallas_call(
        _na_kernel,
        grid=(b, nt),
        in_specs=[own, win(0), win(1), win(2), ctx, win(0), win(1), win(2), ctx,
                  pl.BlockSpec((1, NA_HEADS, TM, 3 * TM), lambda bi, i: (kind(i), 0, 0, 0))],
        out_specs=own,
        out_shape=jax.ShapeDtypeStruct((b, t, w), BF16),
        compiler_params=_cparams(("arbitrary", "arbitrary"), 40),
        name="na_attn",
    )(q, k, k, k, k, v, v, v, v, bias)


def _attend(n_maps, s_lat, tk, q_ref, kt_ref, v_ref, s_ref, emit):
    tq = q_ref.shape[2]
    t = v_ref.shape[2]
    dv = v_ref.shape[3] // 2
    dk = q_ref.shape[3] // n_maps
    maps = range(n_maps)
    qs = [q_ref[0, 0, :, m * dk:(m + 1) * dk] for m in maps]
    init = tuple((jnp.full((tq, 1), NEG_INF, F32), jnp.zeros((tq, 2 * dv), F32)) for _ in maps)

    def scores(slot, j):
        off = j * tk
        mx = []
        for m in maps:
            s = jnp.dot(qs[m], kt_ref[0, 0, m * dk:(m + 1) * dk, pl.ds(off, tk)], preferred_element_type=F32)
            s_ref[m, slot] = s
            mx.append(jnp.max(s, axis=1, keepdims=True))
        return tuple(mx)

    def update(s, mx, v1, carry):
        m_old, acc = carry
        m_new = jnp.maximum(m_old, mx)
        p = jnp.exp(s - m_new).astype(BF16)
        return m_new, jnp.exp(m_old - m_new) * acc + jnp.dot(p, v1, preferred_element_type=F32)

    def absorb(slot, j, mx, carry):
        v1 = v_ref[0, 0, pl.ds(j * tk, tk), :]
        return tuple(update(s_ref[m, slot], mx[m], v1, carry[m]) for m in maps)

    def ctx_scores():
        return [jnp.dot(qs[m], kt_ref[0, 0, m * dk:(m + 1) * dk, s_lat:t], preferred_element_type=F32) for m in maps]

    def finish(sc, carry):
        vc = v_ref[0, 0, s_lat:t, :]
        outs = []
        for m in maps:
            _, acc = update(sc[m], jnp.max(sc[m], axis=1, keepdims=True), vc, carry[m])
            outs.append(acc[:, :dv] / acc[:, dv:dv + 1])
        emit(outs)

    if s_ref is None:
        finish(ctx_scores(), init)
        return

    n_steps = s_lat // tk
    carry, mx = init, scores(0, 0)
    for j in range(n_steps - 1):
        mx_next = scores((j + 1) % 2, j + 1)
        carry = absorb(j % 2, j, mx, carry)
        mx = mx_next
    sc = ctx_scores()
    finish(sc, absorb((n_steps - 1) % 2, n_steps - 1, mx, carry))


def _mla_kernel(s_lat, tk, q_ref, kt_ref, v_ref, o_ref, s_ref=None):
    def emit(outs):
        o_ref[0, 0] = outs[0].astype(BF16)

    _attend(1, s_lat, tk, q_ref, kt_ref, v_ref, s_ref, emit)


def _diff_kernel(s_lat, tk, q_ref, kt_ref, v_ref, lam_ref, g_ref, o_ref, s_ref=None, *, lam_init):
    def emit(outs):
        lam = lam_ref[...]
        lam_full = (jnp.exp(jnp.sum(lam[0:1] * lam[1:2], axis=1, keepdims=True))
                    - jnp.exp(jnp.sum(lam[2:3] * lam[3:4], axis=1, keepdims=True)) + lam_init)
        d = _rms(outs[0] - lam_full * outs[1], g_ref[...]) * (1.0 - lam_init)
        o_ref[0, 0] = d.astype(BF16)

    _attend(2, s_lat, tk, q_ref, kt_ref, v_ref, s_ref, emit)


def _with_ones(v):
    pad = jnp.zeros(v.shape[:-1] + (v.shape[-1] - 1,), v.dtype)
    return jnp.concatenate([v, jnp.ones(v.shape[:-1] + (1,), v.dtype), pad], axis=-1)


def _global_attention(kernel, name, n_maps, q, kt, v, extra, s_lat):
    b, h, t, dq = q.shape
    dv = v.shape[-1]
    dk = kt.shape[2]
    v = _with_ones(v)
    tk = min(1024, s_lat)
    tq = ATTN_TQ if s_lat % ATTN_TQ == 0 else TM
    lc = t - s_lat
    c_blk = s_lat // lc
    params = _cparams(("arbitrary", "arbitrary", "arbitrary"), 48)
    extra_specs = [pl.BlockSpec(a.shape, lambda bi, hi, i: (0, 0)) for a in extra]
    lat = pl.pallas_call(
        functools.partial(kernel, s_lat, tk),
        grid=(b, h, s_lat // tq),
        in_specs=[pl.BlockSpec((1, 1, tq, dq), lambda bi, hi, i: (bi, hi, i, 0)),
                  pl.BlockSpec((1, 1, dk, t), lambda bi, hi, i: (bi, hi, 0, 0)),
                  pl.BlockSpec((1, 1, t, 2 * dv), lambda bi, hi, i: (bi, hi, 0, 0))] + extra_specs,
        out_specs=pl.BlockSpec((1, 1, tq, dv), lambda bi, hi, i: (bi, hi, i, 0)),
        out_shape=jax.ShapeDtypeStruct((b, h, s_lat, dv), BF16),
        scratch_shapes=[pltpu.VMEM((n_maps, 2, tq, tk), F32)],
        compiler_params=params,
        name=name,
    )(q, kt, v, *extra)
    ctx = pl.pallas_call(
        functools.partial(kernel, 0, tk),
        grid=(b, h, 1),
        in_specs=[pl.BlockSpec((1, 1, lc, dq), lambda bi, hi, i: (bi, hi, c_blk, 0)),
                  pl.BlockSpec((1, 1, dk, lc), lambda bi, hi, i: (bi, hi, 0, c_blk)),
                  pl.BlockSpec((1, 1, lc, 2 * dv), lambda bi, hi, i: (bi, hi, c_blk, 0))] + extra_specs,
        out_specs=pl.BlockSpec((1, 1, lc, dv), lambda bi, hi, i: (bi, hi, 0, 0)),
        out_shape=jax.ShapeDtypeStruct((b, h, lc, dv), BF16),
        compiler_params=params,
        name=name + "_ctx",
    )(q, kt, v, *extra)
    return jnp.concatenate([lat, ctx], axis=2)


def _s5_tables(lam_re, lam_im, log_step, b_re, b_im, c_re, c_im):
    c = S5_CHUNK
    lr, li = lam_re.astype(F32), lam_im.astype(F32)
    step = jnp.exp(log_step.astype(F32))[..., None]
    mag = jnp.exp(lr * step)
    ar, ai = mag * jnp.cos(li * step), mag * jnp.sin(li * step)
    den = lr * lr + li * li
    fr = ((ar - 1.0) * lr + ai * li) / den
    fi = (ai * lr - (ar - 1.0) * li) / den
    br, bi = b_re.astype(F32), b_im.astype(F32)
    bbr = fr[..., None] * br - fi[..., None] * bi
    bbi = fr[..., None] * bi + fi[..., None] * br
    tau = jnp.arange(c + 1, dtype=F32)[:, None, None, None]
    pmag = jnp.exp(lr * step * tau)
    pr, pi = pmag * jnp.cos(li * step * tau), pmag * jnp.sin(li * step * tau)
    cr, ci = c_re.astype(F32), c_im.astype(F32)
    abr = pr[..., None] * bbr - pi[..., None] * bbi
    abi = pr[..., None] * bbi + pi[..., None] * bbr
    kk = jnp.einsum('dgip,tdgpj->tdgij', cr, abr) - jnp.einsum('dgip,tdgpj->tdgij', ci, abi)
    s_idx = jnp.arange(c)[:, None]
    t_idx = jnp.arange(c)[None, :]
    lag_f = jnp.clip(t_idx - s_idx, 0, c)
    lag_b = jnp.clip(s_idx - t_idx, 0, c)
    tf = jnp.where((s_idx <= t_idx)[:, :, None, None, None], kk[lag_f, 0], 0.0)
    tb = jnp.where((s_idx >= t_idx)[:, :, None, None, None], kk[lag_b, 1], 0.0)
    g, j_ch = S5_GROUPS, S5_GROUP_CH
    tmat = (tf + tb).transpose(2, 0, 4, 1, 3).reshape(g, c * j_ch, c * j_ch)
    pow_f = jnp.arange(c - 1, -1, -1)
    pow_b = jnp.arange(c)

    def exit_map(d, pw):
        re = abr[pw, d].transpose(1, 0, 3, 2).reshape(g, c * j_ch, S5_STATE)
        im = abi[pw, d].transpose(1, 0, 3, 2).reshape(g, c * j_ch, S5_STATE)
        return jnp.concatenate([re, im], axis=-1)

    bsum = jnp.stack([exit_map(0, pow_f), exit_map(1, pow_b)])

    def entry_map(d, pw):
        p_r, p_i = pr[pw, d], pi[pw, d]
        on_re = jnp.einsum('gip,tgp->gpti', cr[d], p_r) - jnp.einsum('gip,tgp->gpti', ci[d], p_i)
        on_im = -jnp.einsum('gip,tgp->gpti', cr[d], p_i) - jnp.einsum('gip,tgp->gpti', ci[d], p_r)
        return jnp.concatenate([on_re, on_im], axis=1).reshape(g, 2 * S5_STATE, c * j_ch)

    cin = jnp.stack([entry_map(0, jnp.arange(1, c + 1)), entry_map(1, jnp.arange(c, 0, -1))])

    def pair_diag(a):
        a = a.reshape(a.shape[:-3] + (g // 2, 2) + a.shape[-2:])
        z = jnp.zeros_like(a[..., 0, :, :])
        return jnp.concatenate([jnp.concatenate([a[..., 0, :, :], z], axis=-1),
                                jnp.concatenate([z, a[..., 1, :, :]], axis=-1)], axis=-2).astype(BF16)

    a_c = tuple(a.reshape(2, 1, g * S5_STATE) for a in (pr[c], pi[c]))
    return (pair_diag(tmat), pair_diag(bsum[..., :S5_STATE]), pair_diag(bsum[..., S5_STATE:]),
            pair_diag(cin[..., :S5_STATE, :]), pair_diag(cin[..., S5_STATE:, :]), a_c)


def _s5_intra_kernel(u_ref, tm_ref, bre_ref, bim_ref, y_ref, sr_ref, si_ref):
    u = u_ref[0, 0]
    y_ref[0, 0] = jnp.dot(u, tm_ref[0], preferred_element_type=F32)
    for d in range(2):
        sr_ref[0, d] = jnp.dot(u, bre_ref[d, 0], preferred_element_type=F32)
        si_ref[0, d] = jnp.dot(u, bim_ref[d, 0], preferred_element_type=F32)


def _s5_intra(ug, tmat, bre, bim):
    b, gp, nc, w = ug.shape
    lanes = S5_GROUPS * S5_STATE
    summ = pl.BlockSpec((1, 2, nc, 2 * S5_STATE), lambda bi, pi: (bi, 0, 0, pi))
    return pl.pallas_call(
        _s5_intra_kernel,
        grid=(b, gp),
        in_specs=[pl.BlockSpec((1, 1, nc, w), lambda bi, pi: (bi, pi, 0, 0)),
                  pl.BlockSpec((1, w, w), lambda bi, pi: (pi, 0, 0)),
                  pl.BlockSpec((2, 1, w, 2 * S5_STATE), lambda bi, pi: (0, pi, 0, 0)),
                  pl.BlockSpec((2, 1, w, 2 * S5_STATE), lambda bi, pi: (0, pi, 0, 0))],
        out_specs=[pl.BlockSpec((1, 1, nc, w), lambda bi, pi: (bi, pi, 0, 0)), summ, summ],
        out_shape=[jax.ShapeDtypeStruct((b, gp, nc, w), F32),
                   jax.ShapeDtypeStruct((b, 2, nc, lanes), F32),
                   jax.ShapeDtypeStruct((b, 2, nc, lanes), F32)],
        compiler_params=_cparams(("arbitrary", "arbitrary")),
        name="s5_intra",
    )(ug, tmat, bre, bim)


def _s5_scan_kernel(sfr, sfi, sbr, sbi, ar_ref, ai_ref, hfr, hfi, hbr, hbi, st):
    @pl.when(pl.program_id(0) == 0)
    def _():
        st[...] = jnp.zeros_like(st)

    kb = sfr.shape[2]
    afr, afi, abr, abi = ar_ref[0], ai_ref[0], ar_ref[1], ai_ref[1]
    fr, fi, br, bi = st[0], st[1], st[2], st[3]
    for k in range(kb):
        up = slice(k, k + 1)
        dn = slice(kb - 1 - k, kb - k)
        hfr[:, up, :] = fr
        hfi[:, up, :] = fi
        hbr[:, dn, :] = br
        hbi[:, dn, :] = bi
        fr, fi = afr * fr - afi * fi + sfr[:, 0, up, :], afr * fi + afi * fr + sfi[:, 0, up, :]
        br, bi = abr * br - abi * bi + sbr[:, 0, dn, :], abr * bi + abi * br + sbi[:, 0, dn, :]
    st[0], st[1], st[2], st[3] = fr, fi, br, bi


def _s5_scan(sr, si, ar, ai, n_lat_blocks):
    b, _, nc, lanes = sr.shape
    kb = S5_CHUNK

    def f_blk(j):
        return jnp.where(j == 0, n_lat_blocks, j - 1)

    def b_blk(j):
        return jnp.where(j == 0, n_lat_blocks, n_lat_blocks - j)

    def summ(d, blk):
        return pl.BlockSpec((b, 1, kb, lanes), lambda j: (0, d, blk(j), 0))

    def state(blk):
        return pl.BlockSpec((b, kb, lanes), lambda j: (0, blk(j), 0))

    par = pl.BlockSpec((2, 1, lanes), lambda j: (0, 0, 0))
    return pl.pallas_call(
        _s5_scan_kernel,
        grid=(nc // kb,),
        in_specs=[summ(0, f_blk), summ(0, f_blk), summ(1, b_blk), summ(1, b_blk), par, par],
        out_specs=[state(f_blk), state(f_blk), state(b_blk), state(b_blk)],
        out_shape=[jax.ShapeDtypeStruct((b, nc, lanes), F32)] * 4,
        scratch_shapes=[pltpu.VMEM((4, b, 1, lanes), F32)],
        compiler_params=_cparams(("arbitrary",)),
        name="s5_scan",
    )(sr, si, sr, si, ar, ai)


def _s5_inter_kernel(y_ref, hfr, hfi, hbr, hbi, cre_ref, cim_ref, o_ref):
    y = y_ref[0, 0]
    for d, (hr, hi) in enumerate(((hfr, hfi), (hbr, hbi))):
        y = y + jnp.dot(hr[0].astype(BF16), cre_ref[d, 0], preferred_element_type=F32)
        y = y + jnp.dot(hi[0].astype(BF16), cim_ref[d, 0], preferred_element_type=F32)
    o_ref[0, 0] = y.astype(o_ref.dtype)


def _s5_inter(y_intra, states, cre, cim):
    b, gp, nc, w = y_intra.shape
    own = pl.BlockSpec((1, 1, nc, w), lambda bi, pi: (bi, pi, 0, 0))
    st = pl.BlockSpec((1, nc, 2 * S5_STATE), lambda bi, pi: (bi, 0, pi))
    tab = pl.BlockSpec((2, 1, 2 * S5_STATE, w), lambda bi, pi: (0, pi, 0, 0))
    return pl.pallas_call(
        _s5_inter_kernel,
        grid=(b, gp),
        in_specs=[own, st, st, st, st, tab, tab],
        out_specs=own,
        out_shape=jax.ShapeDtypeStruct((b, gp, nc, w), BF16),
        compiler_params=_cparams(("arbitrary", "arbitrary")),
        name="s5_inter",
    )(y_intra, *states, cre, cim)


def _s5_mixer(u, tables, s_lat):
    tmat, bre, bim, cre, cim, (acr, aci) = tables
    b, t, w = u.shape
    c, gp, jc = S5_CHUNK, S5_GROUPS // 2, S5_GROUP_CH
    nc = t // c
    assert (t - s_lat) == c * c and s_lat % (c * c) == 0
    ug = u.reshape(b, nc, c, gp, 2, jc).transpose(0, 3, 1, 4, 2, 5).reshape(b, gp, nc, 2 * c * jc).astype(BF16)
    y_intra, sr, si = _s5_intra(ug, tmat, bre, bim)
    states = _s5_scan(sr, si, acr, aci, s_lat // (c * c))
    y = _s5_inter(y_intra, states, cre, cim)
    return y.reshape(b, gp, nc, 2, c, jc).transpose(0, 2, 4, 1, 3, 5).reshape(b, t, w)


def _merge_kernel(ya, yb, ym, ys, su, dd, wglu, bglu, gates, wb, wo, x_ref, gt1, g2, sh2, sc2, wr, eb,
                  xo_ref, hp_ref, idx_ref, gw_ref, hist_ref):
    d = x_ref.shape[-1]
    y5 = ys[0] + dd[...] * su[0]
    gl = jax.nn.gelu(y5)
    yd = gl * jax.nn.sigmoid(jnp.dot(gl, wglu[...], preferred_element_type=F32) + bglu[...])
    branches = (ya[0], yb[0], ym[0], yd.astype(BF16))
    m = None
    for i in range(N_BRANCH):
        term = gates[0, :, i * d:(i + 1) * d].astype(F32) * jnp.dot(branches[i], wb[i], preferred_element_type=F32)
        m = term if m is None else m + term
    xn = x_ref[0] + gt1[0] * jnp.dot(m.astype(BF16), wo[...], preferred_element_type=F32)
    xo_ref[0] = xn
    h2 = _rms(xn, g2[...]) * (1.0 + sc2[0]) + sh2[0]
    hp_ref[0] = _pack_bf16_pair(h2)

    scores = jax.nn.sigmoid(jnp.dot(h2, wr[...], preferred_element_type=F32, precision=HIGHEST))
    sel = scores + eb[...]
    lane = lax.broadcasted_iota(I32, sel.shape, 1).astype(F32)
    idx_acc = jnp.zeros(sel.shape, F32)
    gw_acc = jnp.zeros(sel.shape, F32)
    chosen = jnp.zeros(sel.shape, F32)
    for k in range(TOP_K):
        mx = jnp.max(sel, axis=1, keepdims=True)
        ik = jnp.min(jnp.where(sel == mx, lane, float(LANES)), axis=1, keepdims=True)
        hit = lane == ik
        gk = jnp.sum(jnp.where(hit, scores, 0.0), axis=1, keepdims=True)
        idx_acc = jnp.where(lane == k, ik, idx_acc)
        gw_acc = jnp.where(lane == k, gk, gw_acc)
        chosen = jnp.where(hit, 1.0, chosen)
        sel = jnp.where(hit, -jnp.inf, sel)
    idx_ref[0] = idx_acc.astype(I32)
    gw_ref[0] = gw_acc / jnp.sum(gw_acc, axis=1, keepdims=True) * ROUTED_SCALE
    hist_ref[0, 0] = jnp.sum(chosen, axis=0, keepdims=True)


def _merge(ya, yb, ym, ys, su, dd, wglu, bglu, gates, wb, wo, x_all, gt1, g2, sh2, sc2, wr, eb, n_lat):
    b, t, d = x_all.shape
    mod_spec = pl.BlockSpec((1, 1, d), lambda bi, i: (bi * 2 + (i >= n_lat).astype(I32), 0, 0))

    def full(a):
        return pl.BlockSpec(a.shape, lambda bi, i: (0,) * a.ndim)

    def tile(w):
        return pl.BlockSpec((1, TM, w), lambda bi, i: (bi, i, 0))

    return pl.pallas_call(
        _merge_kernel,
        grid=(b, t // TM),
        in_specs=[tile(256), tile(256), tile(256), tile(256), tile(256), full(dd), full(wglu), full(bglu),
                  tile(N_BRANCH * d), full(wb), full(wo), tile(d), mod_spec, full(g2), mod_spec, mod_spec,
                  full(wr), full(eb)],
        out_specs=[tile(d), tile(d // 2), tile(LANES), tile(LANES),
                   pl.BlockSpec((1, 1, 1, LANES), lambda bi, i: (bi, i, 0, 0))],
        out_shape=[jax.ShapeDtypeStruct((b, t, d), F32), jax.ShapeDtypeStruct((b, t, d // 2), I32),
                   jax.ShapeDtypeStruct((b, t, LANES), I32), jax.ShapeDtypeStruct((b, t, LANES), F32),
                   jax.ShapeDtypeStruct((b, t // TM, 1, LANES), F32)],
        compiler_params=_cparams(("arbitrary", "arbitrary"), 48),
        name="merge",
    )(ya, yb, ym, ys, su, dd, wglu, bglu, gates, wb, wo, x_all, gt1, g2, sh2, sc2, wr, eb)


def _route_tables(hist, n_assign):
    h = hist.reshape(-1, LANES).astype(I32)
    counts = jnp.sum(h, axis=0)
    padded = (counts + EXPERT_BLOCK - 1) // EXPERT_BLOCK * EXPERT_BLOCK
    pends = jnp.cumsum(padded)
    base = (pends - padded)[None, :] + jnp.cumsum(h, axis=0) - h
    n_blocks = -(-n_assign // EXPERT_BLOCK) + N_EXPERTS
    block_start = jnp.arange(n_blocks, dtype=I32) * EXPERT_BLOCK
    blk_e = jnp.minimum(jnp.sum((pends[None, :N_EXPERTS] <= block_start[:, None]).astype(I32), axis=1),
                        N_EXPERTS - 1)
    return base.astype(F32).reshape(hist.shape), blk_e, n_blocks


def _pos_kernel(idx_ref, base_ref, pos_ref):
    idx = idx_ref[0]
    lane = lax.broadcasted_iota(I32, idx.shape, 1)
    hits = [lane == idx[:, k:k + 1] for k in range(TOP_K)]
    chosen = jnp.zeros(idx.shape, F32)
    for hit in hits:
        chosen = jnp.where(hit, 1.0, chosen)
    row = lax.broadcasted_iota(I32, (TM, TM), 0)
    col = lax.broadcasted_iota(I32, (TM, TM), 1)
    earlier = jnp.where(row > col, 1.0, 0.0).astype(BF16)
    slot = jnp.dot(earlier, chosen.astype(BF16), preferred_element_type=F32) + base_ref[0, 0]
    pos = jnp.zeros(idx.shape, F32)
    for k, hit in enumerate(hits):
        pos = jnp.where(lane == k, jnp.sum(jnp.where(hit, slot, 0.0), axis=1, keepdims=True), pos)
    pos_ref[0] = pos.astype(I32)


def _positions(idx, base):
    b, t, _ = idx.shape
    tile = pl.BlockSpec((1, TM, LANES), lambda bi, i: (bi, i, 0))
    return pl.pallas_call(
        _pos_kernel,
        grid=(b, t // TM),
        in_specs=[tile, pl.BlockSpec((1, 1, 1, LANES), lambda bi, i: (bi, i, 0, 0))],
        out_specs=tile,
        out_shape=jax.ShapeDtypeStruct((b, t, LANES), I32),
        compiler_params=_cparams(("arbitrary", "arbitrary")),
        name="moe_pos",
    )(idx, base)


def _scatter_kernel(pos_ref, hp_ref, init_ref, xs_ref, sem):
    del init_ref
    for j in range(TOP_K * TM):
        pltpu.make_async_copy(hp_ref.at[pl.ds(j % TM, 1)], xs_ref.at[pl.ds(pos_ref[0, 0, j], 1)], sem).start()
    for k in range(TOP_K):
        pltpu.make_async_copy(hp_ref, xs_ref.at[pl.ds(0, TM)], sem).wait()


def _tile_major(pos):
    nt = pos.shape[0] // TM
    return pos[:, :TOP_K].reshape(nt, TM, TOP_K).transpose(0, 2, 1).reshape(nt, 1, TOP_K * TM)


def _scatter_rows(pos3, hp, n_slots):
    n, half = hp.shape
    return pl.pallas_call(
        _scatter_kernel,
        grid=(n // TM,),
        in_specs=[pl.BlockSpec((1, 1, TOP_K * TM), lambda i: (i, 0, 0), memory_space=pltpu.SMEM),
                  pl.BlockSpec((TM, half), lambda i: (i, 0)),
                  pl.BlockSpec(memory_space=pl.ANY)],
        out_specs=pl.BlockSpec(memory_space=pl.ANY),
        out_shape=jax.ShapeDtypeStruct((n_slots, half), hp.dtype),
        scratch_shapes=[pltpu.SemaphoreType.DMA(())],
        input_output_aliases={2: 0},
        compiler_params=_cparams(("arbitrary",)),
        name="moe_scatter",
    )(pos3, hp, jnp.zeros((n_slots, half), hp.dtype))


def _swiglu_packed(p, w1, w3, w2):
    lo, hi = _unpack_bf16_pair(p)
    lo, hi = lo.astype(BF16), hi.astype(BF16)
    n = lo.shape[-1]

    def up(w):
        return (jnp.dot(lo, w[:n], preferred_element_type=F32) + jnp.dot(hi, w[n:], preferred_element_type=F32))

    a = up(w1)
    return jnp.dot((a * jax.nn.sigmoid(a) * up(w3)).astype(BF16), w2, preferred_element_type=F32)


def _ffn_kernel(be_ref, x_ref, w1_ref, w3_ref, w2_ref, y_ref):
    del be_ref
    y_ref[...] = _pack_bf16_pair(_swiglu_packed(x_ref[...], w1_ref[0], w3_ref[0], w2_ref[0]))


def _expert_ffn(blk_e, xs, w1, w3, w2):
    n_slots, half = xs.shape
    e, d, ff = w1.shape
    row = pl.BlockSpec((EXPERT_BLOCK, half), lambda i, be: (i, 0))
    return pl.pallas_call(
        _ffn_kernel,
        grid_spec=pltpu.PrefetchScalarGridSpec(
            num_scalar_prefetch=1,
            grid=(n_slots // EXPERT_BLOCK,),
            in_specs=[row,
                      pl.BlockSpec((1, d, ff), lambda i, be: (be[i], 0, 0)),
                      pl.BlockSpec((1, d, ff), lambda i, be: (be[i], 0, 0)),
                      pl.BlockSpec((1, ff, d), lambda i, be: (be[i], 0, 0))],
            out_specs=row),
        out_shape=jax.ShapeDtypeStruct((n_slots, half), I32),
        compiler_params=_cparams(("arbitrary",)),
        name="moe_ffn",
    )(blk_e, xs, w1, w3, w2)


def _combine_kernel(pos_ref, gw_ref, ys_ref, hp_ref, x_ref, gt2, ws1, ws3, ws2, xo_ref, buf, sem):
    for j in range(TOP_K * TM):
        pltpu.make_async_copy(ys_ref.at[pl.ds(pos_ref[0, 0, j], 1)], buf.at[j // TM, pl.ds(j % TM, 1)], sem).start()

    shared = _swiglu_packed(hp_ref[...], ws1[...], ws3[...], ws2[...])

    for k in range(TOP_K):
        pltpu.make_async_copy(ys_ref.at[pl.ds(0, TM)], buf.at[k], sem).wait()

    gw = gw_ref[...]
    lo = hi = None
    for k in range(TOP_K):
        wk = gw[:, k:k + 1]
        lk, hk = _unpack_bf16_pair(buf[k])
        lo = wk * lk if lo is None else lo + wk * lk
        hi = wk * hk if hi is None else hi + wk * hk
    xo_ref[...] = x_ref[...] + gt2[0] * (shared + jnp.concatenate([lo, hi], axis=1))


def _combine(pos3, gw, ys, hp, x_flat, gt2, ws1, ws3, ws2, tiles_per_batch, n_lat):
    n, d = x_flat.shape
    nt = n // TM

    def full(a):
        return pl.BlockSpec(a.shape, lambda i: (0,) * a.ndim)

    def tile(w):
        return pl.BlockSpec((TM, w), lambda i: (i, 0))

    def mod_idx(i):
        return (i // tiles_per_batch) * 2 + ((i % tiles_per_batch) >= n_lat).astype(I32)

    return pl.pallas_call(
        _combine_kernel,
        grid=(nt,),
        in_specs=[pl.BlockSpec((1, 1, TOP_K * TM), lambda i: (i, 0, 0), memory_space=pltpu.SMEM),
                  tile(LANES), pl.BlockSpec(memory_space=pl.ANY), tile(d // 2), tile(d),
                  pl.BlockSpec((1, 1, d), lambda i: (mod_idx(i), 0, 0)), full(ws1), full(ws3), full(ws2)],
        out_specs=tile(d),
        out_shape=jax.ShapeDtypeStruct((n, d), F32),
        scratch_shapes=[pltpu.VMEM((TOP_K, TM, d // 2), I32), pltpu.SemaphoreType.DMA(())],
        compiler_params=_cparams(("arbitrary",), 40),
        name="moe_combine",
    )(pos3, gw, ys, hp, x_flat, gt2, ws1, ws3, ws2)


def _final_kernel(x_ref, g_ref, o_ref):
    o_ref[0] = _rms(x_ref[0], g_ref[...])


def _final_norm(x_all, g, s_lat):
    b, _, d = x_all.shape
    return pl.pallas_call(
        _final_kernel,
        grid=(b, s_lat // TM),
        in_specs=[pl.BlockSpec((1, TM, d), lambda bi, i: (bi, i, 0)), pl.BlockSpec((1, d), lambda bi, i: (0, 0))],
        out_specs=pl.BlockSpec((1, TM, d), lambda bi, i: (bi, i, 0)),
        out_shape=jax.ShapeDtypeStruct((b, s_lat, d), F32),
        compiler_params=_cparams(("arbitrary", "arbitrary")),
        name="final_norm",
    )(x_all, g.reshape(1, d))


def _rope_tables(s_lat, t):
    pos = jnp.arange(s_lat, dtype=I32)
    half = 8
    freqs = ROPE_BASE ** (-jnp.arange(half, dtype=F32) * 2.0 / (2 * half))

    def part(p):
        ang = p.astype(F32)[:, None] * freqs[None, :]
        c, s = jnp.cos(ang), jnp.sin(ang)
        return jnp.concatenate([c, c], axis=1), jnp.concatenate([-s, s], axis=1)

    cr, sr = part(pos // GRID_W)
    cc, sc = part(pos % GRID_W)
    cos = jnp.tile(jnp.concatenate([cr, cc], axis=1), (1, 8))
    sin = jnp.tile(jnp.concatenate([sr, sc], axis=1), (1, 8))
    pad = t - s_lat
    return (jnp.concatenate([cos, jnp.ones((pad, 256), F32)], axis=0),
            jnp.concatenate([sin, jnp.zeros((pad, 256), F32)], axis=0))


def _heads(x, h):
    b, t, w = x.shape
    return x.reshape(b, t, h, w // h).transpose(0, 2, 1, 3)


def _tokens(o):
    b, h, t, d = o.shape
    return o.transpose(0, 2, 1, 3).reshape(b, t, h * d)


def kernel(x, c, ctx, c_ctx, w_ada, b_ada, g_norm1, g_norm2, w_in, na_rpb, diff_lambda, g_diff, g_mla_q, g_mla_kv, w_mla_uq, w_mla_ukv, s5_lam_re, s5_lam_im, s5_log_step, s5_b_re, s5_b_im, s5_c_re, s5_c_im, s5_d, w_glu, b_glu, w_branch, w_out, w_router, e_bias, w_e1, w_e3, w_e2, w_s1, w_s3, w_s2, g_final):
    b, s_lat, d = x.shape
    lc = ctx.shape[1]
    t = s_lat + lc
    depth = w_ada.shape[0]
    n_lat = s_lat // TM
    assert lc == TM and s_lat % TM == 0 and n_lat >= 3 and b + 1 <= 8 and d % 256 == 0
    tiles_per_batch = t // TM

    x_all = jnp.concatenate([x, ctx], axis=1)
    c8 = jnp.zeros((8, d), F32).at[:b].set(c).at[b].set(c_ctx)
    ada = _ada(c8, w_ada, b_ada)
    cos_t, sin_t = _rope_tables(s_lat, t)

    hq = jnp.arange(MLA_HEADS)[:, None] * (MLA_NOPE + MLA_ROPE)
    uq_cols = jnp.concatenate([(hq + jnp.arange(MLA_NOPE)[None]).reshape(-1),
                               (hq + MLA_NOPE + jnp.arange(MLA_ROPE)[None]).reshape(-1)])
    hk = jnp.arange(MLA_HEADS)[:, None] * (MLA_NOPE + MLA_V)
    ukv_cols = jnp.concatenate([(hk + jnp.arange(MLA_NOPE)[None]).reshape(-1),
                                (hk + MLA_NOPE + jnp.arange(MLA_V)[None]).reshape(-1)])
    kpe_end = 1952

    for l in range(depth):
        mods = jnp.concatenate([ada[l, :b, None, :], jnp.broadcast_to(ada[l, b], (b, 1, 6 * d))], axis=1)
        sh1, sc1, gt1, sh2, sc2, gt2 = (mods[:, :, i * d:(i + 1) * d].reshape(2 * b, 1, d) for i in range(6))
        w_main = jnp.concatenate([w_in[l][:, :kpe_end], jnp.zeros((d, OFF_U - kpe_end), F32),
                                  w_in[l][:, kpe_end:]], axis=1).astype(BF16)
        (naq, nak, nav, dfq, dfqr, dfk, dfkr, dfv, mlqn, mlqp, mlqpr, mlkn, mlv, mlkx, s5u, gates) = _proj(
            x_all, g_norm1[l].reshape(1, d), sh1, sc1, w_main, cos_t, sin_t, g_mla_q[l].reshape(1, -1),
            g_mla_kv[l].reshape(1, -1), w_mla_uq[l][:, uq_cols].astype(BF16), w_mla_ukv[l][:, ukv_cols].astype(BF16),
            n_lat)

        ya = _na(naq, nak, nav, _na_bias(na_rpb[l]), n_lat)

        lam_init = 0.8 - 0.6 * math.exp(-0.3 * l)
        dq = jnp.concatenate([_heads(dfqr, 2 * DIFF_HEADS), _heads(dfq, 2 * DIFF_HEADS)], axis=-1)
        dq = dq.reshape(b, DIFF_HEADS, 2, t, 2 * DIFF_DK).transpose(0, 1, 3, 2, 4).reshape(b, DIFF_HEADS, t, -1)
        dk = jnp.concatenate([_heads(dfkr, 2 * DIFF_HEADS), _heads(dfk, 2 * DIFF_HEADS)], axis=-1)
        dkt = dk.transpose(0, 1, 3, 2).reshape(b, DIFF_HEADS, 4 * DIFF_DK, t)
        yb = _tokens(_global_attention(
            functools.partial(_diff_kernel, lam_init=lam_init), "diff_attn", 2, dq, dkt, _heads(dfv, DIFF_HEADS),
            (diff_lambda[l], g_diff[l].reshape(1, -1)), s_lat))

        mq = jnp.concatenate([_heads(mlqn, MLA_HEADS), _heads(mlqpr, MLA_HEADS), _heads(mlqp, MLA_HEADS)], axis=-1)
        kx = jnp.broadcast_to(mlkx[:, None, :, :2 * MLA_ROPE], (b, MLA_HEADS, t, 2 * MLA_ROPE))
        mkt = jnp.concatenate([_heads(mlkn, MLA_HEADS), kx], axis=-1).transpose(0, 1, 3, 2)
        ym = _tokens(_global_attention(_mla_kernel, "mla_attn", 1, mq, mkt, _heads(mlv, MLA_HEADS), (), s_lat))

        ys = _s5_mixer(s5u, _s5_tables(s5_lam_re[l], s5_lam_im[l], s5_log_step[l], s5_b_re[l], s5_b_im[l],
                                       s5_c_re[l], s5_c_im[l]), s_lat)

        wr = jnp.concatenate([w_router[l], jnp.zeros((d, LANES - N_EXPERTS), F32)], axis=1)
        eb = jnp.concatenate([e_bias[l].astype(F32), jnp.full((LANES - N_EXPERTS,), -jnp.inf, F32)]).reshape(1, LANES)
        x_all, hp, idx, gw, hist = _merge(
            ya, yb, ym, ys, s5u, s5_d[l].reshape(1, -1), w_glu[l], b_glu[l].reshape(1, -1), gates,
            w_branch[l].astype(BF16), w_out[l].astype(BF16), x_all, gt1, g_norm2[l].reshape(1, d), sh2, sc2, wr, eb,
            n_lat)

        n = b * t
        base, blk_e, n_blocks = _route_tables(hist, n * TOP_K)
        pos3 = _tile_major(_positions(idx, base).reshape(n, LANES))
        hp_flat = hp.reshape(n, d // 2)
        xs = _scatter_rows(pos3, hp_flat, n_blocks * EXPERT_BLOCK)
        y_sorted = _expert_ffn(blk_e, xs, w_e1[l].astype(BF16), w_e3[l].astype(BF16), w_e2[l].astype(BF16))
        x_all = _combine(pos3, gw.reshape(n, LANES), y_sorted, hp_flat, x_all.reshape(n, d), gt2,
                         w_s1[l].astype(BF16), w_s3[l].astype(BF16), w_s2[l].astype(BF16),
                         tiles_per_batch, n_lat).reshape(b, t, d)

    return _final_norm(x_all, g_final, s_lat)
```

```python
import functools
import math

import jax
import jax.numpy as jnp
import numpy as np
from jax import lax
from jax.experimental import pallas as pl
from jax.experimental.pallas import tpu as pltpu

F32 = jnp.float32
BF16 = jnp.bfloat16
I32 = jnp.int32
HIGHEST = lax.Precision.HIGHEST

GRID_W = 64
HEAD_DIM = 64
NA_HEADS = 4
WIN_H = 8
WIN_W = 16
DIFF_HEADS = 4
DIFF_DK = 32
DIFF_DV = 64
MLA_HEADS = 4
MLA_NOPE = 64
MLA_ROPE = 32
MLA_V = 64
S5_GROUPS = 16
S5_GROUP_CH = 16
S5_STATE = 64
S5_WIDTH = S5_GROUPS * S5_GROUP_CH
S5_CHUNK = 16
N_BRANCH = 4
N_EXPERTS = 64
TOP_K = 6
EXPERT_BLOCK = 256
ROUTED_SCALE = 1.0
ROPE_BASE = 10000.0
EPS = 1e-6
NEG_INF = -1e30

TM = 256
ATTN_TQ = 512
LANES = 128
HI_MASK = -65536

NA_SCALE = HEAD_DIM ** -0.5
DIFF_SCALE = DIFF_DK ** -0.5
MLA_SCALE = (MLA_NOPE + MLA_ROPE) ** -0.5

OFF_NA, OFF_DQ, OFF_DK, OFF_DV, OFF_CQ, OFF_CKV, OFF_KPE, OFF_U, OFF_GATE = (
    0, 768, 1024, 1280, 1536, 1792, 1920, 2048, 2304)


def _cparams(sem, vmem_mb=None):
    kw = dict(dimension_semantics=sem)
    if vmem_mb is not None:
        kw["vmem_limit_bytes"] = vmem_mb * 2 ** 20
    return pltpu.CompilerParams(**kw)


def _rms(x, g):
    return x * lax.rsqrt(jnp.mean(x * x, axis=-1, keepdims=True) + EPS) * g


def _rope(x, cos, sin_signed):
    w = x.shape[-1]
    lane = lax.broadcasted_iota(I32, x.shape, 1)
    partner = jnp.where((lane & 8) == 0, pltpu.roll(x, w - 8, 1), pltpu.roll(x, 8, 1))
    return x * cos + partner * sin_signed


def _pack_bf16_pair(x):
    n = x.shape[-1] // 2
    bits = lax.bitcast_convert_type(x.astype(BF16).astype(F32), I32)
    return lax.shift_right_logical(bits[:, :n], 16) | (bits[:, n:] & HI_MASK)


def _unpack_bf16_pair(p):
    lo = lax.bitcast_convert_type(lax.shift_left(p, 16), F32)
    hi = lax.bitcast_convert_type(p & HI_MASK, F32)
    return lo, hi


def _ada_kernel(c_ref, w_ref, b_ref, o_ref):
    c = c_ref[...]
    s = c * jax.nn.sigmoid(c)
    o_ref[0] = jnp.dot(s, w_ref[0], preferred_element_type=F32, precision=HIGHEST) + b_ref[0]


def _ada(c8, w_ada, b_ada):
    depth, d, n = w_ada.shape
    tn = 1536
    return pl.pallas_call(
        _ada_kernel,
        grid=(depth, n // tn),
        in_specs=[pl.BlockSpec((8, d), lambda l, j: (0, 0)),
                  pl.BlockSpec((1, d, tn), lambda l, j: (l, 0, j)),
                  pl.BlockSpec((1, 1, tn), lambda l, j: (l, 0, j))],
        out_specs=pl.BlockSpec((1, 8, tn), lambda l, j: (l, 0, j)),
        out_shape=jax.ShapeDtypeStruct((depth, 8, n), F32),
        compiler_params=_cparams(("arbitrary", "arbitrary"), 40),
        name="ada",
    )(c8, w_ada, b_ada.reshape(depth, 1, n))


def _proj_kernel(n_lat, x_ref, g_ref, sh_ref, sc_ref, w_ref, cos_ref, sin_ref, gq_ref, gkv_ref, wuq_ref, wukv_ref,
                 naq, nak, nav, dfq, dfqr, dfk, dfkr, dfv, mlqn, mlqp, mlqpr, mlkn, mlv, mlkx, s5u, gates):
    is_ctx = pl.program_id(1) >= n_lat
    m_lat = jnp.where(is_ctx, 0.0, 1.0).astype(F32)
    m_ctx = 1.0 - m_lat
    h = (_rms(x_ref[0], g_ref[...]) * (1.0 + sc_ref[0]) + sh_ref[0]).astype(BF16)

    def proj(off, width):
        return jnp.dot(h, w_ref[:, off:off + width], preferred_element_type=F32)

    cos = cos_ref[...]
    sin = sin_ref[...]
    a = proj(OFF_NA, 768)
    naq[0] = (a[:, :256] * NA_SCALE).astype(BF16)
    nak[0] = a[:, 256:512].astype(BF16)
    nav[0] = a[:, 512:].astype(BF16)

    bq = proj(OFF_DQ, 256) * DIFF_SCALE
    dfq[0] = bq.astype(BF16)
    dfqr[0] = _rope(bq, cos, sin).astype(BF16)
    bk = proj(OFF_DK, 256)
    dfk[0] = (bk * m_ctx).astype(BF16)
    dfkr[0] = (_rope(bk, cos, sin) * m_lat).astype(BF16)
    dfv[0] = proj(OFF_DV, 256).astype(BF16)

    cq = _rms(proj(OFF_CQ, 256), gq_ref[...]).astype(BF16)
    q2 = jnp.dot(cq, wuq_ref[...], preferred_element_type=F32) * MLA_SCALE
    qp = q2[:, 256:]
    mlqn[0] = q2[:, :256].astype(BF16)
    mlqp[0] = qp.astype(BF16)
    mlqpr[0] = _rope(qp, cos[:, :LANES], sin[:, :LANES]).astype(BF16)
    ckv = _rms(proj(OFF_CKV, 128), gkv_ref[...]).astype(BF16)
    kv = jnp.dot(ckv, wukv_ref[...], preferred_element_type=F32)
    mlkn[0] = kv[:, :256].astype(BF16)
    mlv[0] = kv[:, 256:].astype(BF16)
    kpe = proj(OFF_KPE, LANES)
    kper = _rope(kpe, cos[:, :LANES], sin[:, :LANES])
    mlkx[0] = (kper * m_lat + pltpu.roll(kpe, MLA_ROPE, 1) * m_ctx).astype(BF16)

    s5u[0] = proj(OFF_U, 256)
    d = x_ref.shape[-1]
    for i in range(N_BRANCH):
        gates[0, :, i * d:(i + 1) * d] = jax.nn.sigmoid(proj(OFF_GATE + i * d, d)).astype(BF16)


def _proj(x_all, g1, shift, scale, w_main, cos_t, sin_t, g_q, g_kv, w_uq, w_ukv, n_lat):
    b, t, d = x_all.shape
    nt = t // TM
    mod_spec = pl.BlockSpec((1, 1, d), lambda bi, i: (bi * 2 + (i >= n_lat).astype(I32), 0, 0))

    def full(a):
        return pl.BlockSpec(a.shape, lambda bi, i: (0,) * a.ndim)

    def tile(w):
        return pl.BlockSpec((1, TM, w), lambda bi, i: (bi, i, 0))

    widths = [256, 256, 256, 256, 256, 256, 256, 256, 256, 128, 128, 256, 256, 128, 256, N_BRANCH * d]
    dtypes = [BF16] * 14 + [F32, BF16]
    return pl.pallas_call(
        functools.partial(_proj_kernel, n_lat),
        grid=(b, nt),
        in_specs=[tile(d), full(g1), mod_spec, mod_spec, full(w_main),
                  pl.BlockSpec((TM, 256), lambda bi, i: (i, 0)), pl.BlockSpec((TM, 256), lambda bi, i: (i, 0)),
                  full(g_q), full(g_kv), full(w_uq), full(w_ukv)],
        out_specs=[tile(w) for w in widths],
        out_shape=[jax.ShapeDtypeStruct((b, t, w), dt) for w, dt in zip(widths, dtypes)],
        compiler_params=_cparams(("arbitrary", "arbitrary"), 56),
        name="proj",
    )(x_all, g1, shift, scale, w_main, cos_t, sin_t, g_q, g_kv, w_uq, w_ukv)


def _na_kernel(q_ref, k0, k1, k2, kc, v0, v1, v2, vc, bias_ref, o_ref):
    nt_dims = (((1,), (1,)), ((), ()))
    outs = []
    for h in range(NA_HEADS):
        sl = slice(h * HEAD_DIM, (h + 1) * HEAD_DIM)
        qh = q_ref[0, :, sl]
        sw = jnp.concatenate(
            [lax.dot_general(qh, kr[0, :, sl], nt_dims, preferred_element_type=F32) for kr in (k0, k1, k2)],
            axis=1) + bias_ref[0, h]
        sc = lax.dot_general(qh, kc[0, :, sl], nt_dims, preferred_element_type=F32)
        m = jnp.maximum(jnp.max(sw, axis=1, keepdims=True), jnp.max(sc, axis=1, keepdims=True))
        pw = jnp.exp(sw - m)
        pc = jnp.exp(sc - m)
        l = jnp.sum(pw, axis=1, keepdims=True) + jnp.sum(pc, axis=1, keepdims=True)
        o = jnp.dot(pc.astype(BF16), vc[0, :, sl], preferred_element_type=F32)
        for j, vr in enumerate((v0, v1, v2)):
            o = o + jnp.dot(pw[:, j * TM:(j + 1) * TM].astype(BF16), vr[0, :, sl], preferred_element_type=F32)
        outs.append(o / l)
    o_ref[0] = jnp.concatenate(outs, axis=1).astype(BF16)


def _na_bias(rpb):
    rows_q = TM // GRID_W
    rows_k = 3 * rows_q
    qr = np.arange(rows_q)
    kr = np.arange(rows_k)
    col = np.arange(GRID_W)
    cstart = np.clip(col - WIN_W // 2, 0, GRID_W - WIN_W)
    col_ok = (col[None, :] >= cstart[:, None]) & (col[None, :] < cstart[:, None] + WIN_W)
    dc = np.clip(col[None, :] - col[:, None], 1 - WIN_W, WIN_W - 1) + (WIN_W - 1)
    sel_c = (dc[:, :, None] == np.arange(2 * WIN_W - 1)).astype(np.float32)
    kinds = []
    for start_rel, q_rel in ((np.zeros_like(qr), qr), (qr, qr + 4), (np.full_like(qr, 4), qr + 8)):
        row_ok = (kr[None, :] >= start_rel[:, None]) & (kr[None, :] < start_rel[:, None] + WIN_H)
        dr = np.clip(kr[None, :] - q_rel[:, None] + (WIN_H - 1), 0, 2 * WIN_H - 2)
        sel_r = (dr[:, :, None] == np.arange(2 * WIN_H - 1)).astype(np.float32)
        bias = jnp.einsum('hab,rsa,cdb->hrcsd', rpb.astype(F32), sel_r, sel_c, precision=HIGHEST)
        ok = row_ok[:, None, :, None] & col_ok[None, :, None, :]
        kinds.append(jnp.where(ok[None], bias, NEG_INF).reshape(NA_HEADS, TM, 3 * TM))
    kinds.append(jnp.full((NA_HEADS, TM, 3 * TM), NEG_INF, F32))
    return jnp.stack(kinds)


def _na(q, k, v, bias, n_lat):
    b, t, w = q.shape
    nt = t // TM

    def base(i):
        return jnp.clip(i - 1, 0, n_lat - 3)

    def kind(i):
        return jnp.where(i >= n_lat, 3, jnp.where(i == 0, 0, jnp.where(i == n_lat - 1, 2, 1)))

    def win(j):
        return pl.BlockSpec((1, TM, w), lambda bi, i: (bi, base(i) + j, 0))

    ctx = pl.BlockSpec((1, TM, w), lambda bi, i: (bi, n_lat, 0))
    own = pl.BlockSpec((1, TM, w), lambda bi, i: (bi, i, 0))
    return pl.pallas_call(
        _na_kernel,
        grid=(b, nt),
        in_specs=[own, win(0), win(1), win(2), ctx, win(0), win(1), win(2), ctx,
                  pl.BlockSpec((1, NA_HEADS, TM, 3 * TM), lambda bi, i: (kind(i), 0, 0, 0))],
        out_specs=own,
        out_shape=jax.ShapeDtypeStruct((b, t, w), BF16),
        compiler_params=_cparams(("arbitrary", "arbitrary"), 40),
        name="na_attn",
    )(q, k, k, k, k, v, v, v, v, bias)


def _attend(n_maps, s_lat, tk, q_ref, kt_ref, v_ref, s_ref, emit):
    tq = q_ref.shape[2]
    t = v_ref.shape[2]
    dv = v_ref.shape[3] // 2
    dk = q_ref.shape[3] // n_maps
    maps = range(n_maps)
    qs = [q_ref[0, 0, :, m * dk:(m + 1) * dk] for m in maps]
    init = tuple((jnp.full((tq, 1), NEG_INF, F32), jnp.zeros((tq, 2 * dv), F32)) for _ in maps)

    def scores(slot, j):
        off = j * tk
        mx = []
        for m in maps:
            s = jnp.dot(qs[m], kt_ref[0, 0, m * dk:(m + 1) * dk, pl.ds(off, tk)], preferred_element_type=F32)
            s_ref[m, slot] = s
            mx.append(jnp.max(s, axis=1, keepdims=True))
        return tuple(mx)

    def update(s, mx, v1, carry):
        m_old, acc = carry
        m_new = jnp.maximum(m_old, mx)
        p = jnp.exp(s - m_new).astype(BF16)
        return m_new, jnp.exp(m_old - m_new) * acc + jnp.dot(p, v1, preferred_element_type=F32)

    def absorb(slot, j, mx, carry):
        v1 = v_ref[0, 0, pl.ds(j * tk, tk), :]
        return tuple(update(s_ref[m, slot], mx[m], v1, carry[m]) for m in maps)

    def ctx_scores():
        return [jnp.dot(qs[m], kt_ref[0, 0, m * dk:(m + 1) * dk, s_lat:t], preferred_element_type=F32) for m in maps]

    def finish(sc, carry):
        vc = v_ref[0, 0, s_lat:t, :]
        outs = []
        for m in maps:
            _, acc = update(sc[m], jnp.max(sc[m], axis=1, keepdims=True), vc, carry[m])
            outs.append(acc[:, :dv] / acc[:, dv:dv + 1])
        emit(outs)

    if s_ref is None:
        finish(ctx_scores(), init)
        return

    n_steps = s_lat // tk
    carry, mx = init, scores(0, 0)
    for j in range(n_steps - 1):
        mx_next = scores((j + 1) % 2, j + 1)
        carry = absorb(j % 2, j, mx, carry)
        mx = mx_next
    sc = ctx_scores()
    finish(sc, absorb((n_steps - 1) % 2, n_steps - 1, mx, carry))


def _mla_kernel(s_lat, tk, q_ref, kt_ref, v_ref, o_ref, s_ref=None):
    def emit(outs):
        o_ref[0, 0] = outs[0].astype(BF16)

    _attend(1, s_lat, tk, q_ref, kt_ref, v_ref, s_ref, emit)


def _diff_kernel(s_lat, tk, q_ref, kt_ref, v_ref, lam_ref, g_ref, o_ref, s_ref=None, *, lam_init):
    def emit(outs):
        lam = lam_ref[...]
        lam_full = (jnp.exp(jnp.sum(lam[0:1] * lam[1:2], axis=1, keepdims=True))
                    - jnp.exp(jnp.sum(lam[2:3] * lam[3:4], axis=1, keepdims=True)) + lam_init)
        d = _rms(outs[0] - lam_full * outs[1], g_ref[...]) * (1.0 - lam_init)
        o_ref[0, 0] = d.astype(BF16)

    _attend(2, s_lat, tk, q_ref, kt_ref, v_ref, s_ref, emit)


def _with_ones(v):
    pad = jnp.zeros(v.shape[:-1] + (v.shape[-1] - 1,), v.dtype)
    return jnp.concatenate([v, jnp.ones(v.shape[:-1] + (1,), v.dtype), pad], axis=-1)


def _global_attention(kernel, name, n_maps, q, kt, v, extra, s_lat):
    b, h, t, dq = q.shape
    dv = v.shape[-1]
    dk = kt.shape[2]
    v = _with_ones(v)
    tk = min(1024, s_lat)
    tq = ATTN_TQ if s_lat % ATTN_TQ == 0 else TM
    lc = t - s_lat
    c_blk = s_lat // lc
    params = _cparams(("arbitrary", "arbitrary", "arbitrary"), 48)
    extra_specs = [pl.BlockSpec(a.shape, lambda bi, hi, i: (0, 0)) for a in extra]
    lat = pl.pallas_call(
        functools.partial(kernel, s_lat, tk),
        grid=(b, h, s_lat // tq),
        in_specs=[pl.BlockSpec((1, 1, tq, dq), lambda bi, hi, i: (bi, hi, i, 0)),
                  pl.BlockSpec((1, 1, dk, t), lambda bi, hi, i: (bi, hi, 0, 0)),
                  pl.BlockSpec((1, 1, t, 2 * dv), lambda bi, hi, i: (bi, hi, 0, 0))] + extra_specs,
        out_specs=pl.BlockSpec((1, 1, tq, dv), lambda bi, hi, i: (bi, hi, i, 0)),
        out_shape=jax.ShapeDtypeStruct((b, h, s_lat, dv), BF16),
        scratch_shapes=[pltpu.VMEM((n_maps, 2, tq, tk), F32)],
        compiler_params=params,
        name=name,
    )(q, kt, v, *extra)
    ctx = pl.pallas_call(
        functools.partial(kernel, 0, tk),
        grid=(b, h, 1),
        in_specs=[pl.BlockSpec((1, 1, lc, dq), lambda bi, hi, i: (bi, hi, c_blk, 0)),
                  pl.BlockSpec((1, 1, dk, lc), lambda bi, hi, i: (bi, hi, 0, c_blk)),
                  pl.BlockSpec((1, 1, lc, 2 * dv), lambda bi, hi, i: (bi, hi, c_blk, 0))] + extra_specs,
        out_specs=pl.BlockSpec((1, 1, lc, dv), lambda bi, hi, i: (bi, hi, 0, 0)),
        out_shape=jax.ShapeDtypeStruct((b, h, lc, dv), BF16),
        compiler_params=params,
        name=name + "_ctx",
    )(q, kt, v, *extra)
    return jnp.concatenate([lat, ctx], axis=2)


def _s5_tables(lam_re, lam_im, log_step, b_re, b_im, c_re, c_im):
    c = S5_CHUNK
    lr, li = lam_re.astype(F32), lam_im.astype(F32)
    step = jnp.exp(log_step.astype(F32))[..., None]
    mag = jnp.exp(lr * step)
    ar, ai = mag * jnp.cos(li * step), mag * jnp.sin(li * step)
    den = lr * lr + li * li
    fr = ((ar - 1.0) * lr + ai * li) / den
    fi = (ai * lr - (ar - 1.0) * li) / den
    br, bi = b_re.astype(F32), b_im.astype(F32)
    bbr = fr[..., None] * br - fi[..., None] * bi
    bbi = fr[..., None] * bi + fi[..., None] * br
    tau = jnp.arange(c + 1, dtype=F32)[:, None, None, None]
    pmag = jnp.exp(lr * step * tau)
    pr, pi = pmag * jnp.cos(li * step * tau), pmag * jnp.sin(li * step * tau)
    cr, ci = c_re.astype(F32), c_im.astype(F32)
    abr = pr[..., None] * bbr - pi[..., None] * bbi
    abi = pr[..., None] * bbi + pi[..., None] * bbr
    kk = jnp.einsum('dgip,tdgpj->tdgij', cr, abr) - jnp.einsum('dgip,tdgpj->tdgij', ci, abi)
    s_idx = jnp.arange(c)[:, None]
    t_idx = jnp.arange(c)[None, :]
    lag_f = jnp.clip(t_idx - s_idx, 0, c)
    lag_b = jnp.clip(s_idx - t_idx, 0, c)
    tf = jnp.where((s_idx <= t_idx)[:, :, None, None, None], kk[lag_f, 0], 0.0)
    tb = jnp.where((s_idx >= t_idx)[:, :, None, None, None], kk[lag_b, 1], 0.0)
    g, j_ch = S5_GROUPS, S5_GROUP_CH
    tmat = (tf + tb).transpose(2, 0, 4, 1, 3).reshape(g, c * j_ch, c * j_ch)
    pow_f = jnp.arange(c - 1, -1, -1)
    pow_b = jnp.arange(c)

    def exit_map(d, pw):
        re = abr[pw, d].transpose(1, 0, 3, 2).reshape(g, c * j_ch, S5_STATE)
        im = abi[pw, d].transpose(1, 0, 3, 2).reshape(g, c * j_ch, S5_STATE)
        return jnp.concatenate([re, im], axis=-1)

    bsum = jnp.stack([exit_map(0, pow_f), exit_map(1, pow_b)])

    def entry_map(d, pw):
        p_r, p_i = pr[pw, d], pi[pw, d]
        on_re = jnp.einsum('gip,tgp->gpti', cr[d], p_r) - jnp.einsum('gip,tgp->gpti', ci[d], p_i)
        on_im = -jnp.einsum('gip,tgp->gpti', cr[d], p_i) - jnp.einsum('gip,tgp->gpti', ci[d], p_r)
        return jnp.concatenate([on_re, on_im], axis=1).reshape(g, 2 * S5_STATE, c * j_ch)

    cin = jnp.stack([entry_map(0, jnp.arange(1, c + 1)), entry_map(1, jnp.arange(c, 0, -1))])

    def pair_diag(a):
        a = a.reshape(a.shape[:-3] + (g // 2, 2) + a.shape[-2:])
        z = jnp.zeros_like(a[..., 0, :, :])
        return jnp.concatenate([jnp.concatenate([a[..., 0, :, :], z], axis=-1),
                                jnp.concatenate([z, a[..., 1, :, :]], axis=-1)], axis=-2).astype(BF16)

    a_c = tuple(a.reshape(2, 1, g * S5_STATE) for a in (pr[c], pi[c]))
    return (pair_diag(tmat), pair_diag(bsum[..., :S5_STATE]), pair_diag(bsum[..., S5_STATE:]),
            pair_diag(cin[..., :S5_STATE, :]), pair_diag(cin[..., S5_STATE:, :]), a_c)


def _s5_intra_kernel(u_ref, tm_ref, bre_ref, bim_ref, y_ref, sr_ref, si_ref):
    u = u_ref[0, 0]
    y_ref[0, 0] = jnp.dot(u, tm_ref[0], preferred_element_type=F32)
    for d in range(2):
        sr_ref[0, d] = jnp.dot(u, bre_ref[d, 0], preferred_element_type=F32)
        si_ref[0, d] = jnp.dot(u, bim_ref[d, 0], preferred_element_type=F32)


def _s5_intra(ug, tmat, bre, bim):
    b, gp, nc, w = ug.shape
    lanes = S5_GROUPS * S5_STATE
    summ = pl.BlockSpec((1, 2, nc, 2 * S5_STATE), lambda bi, pi: (bi, 0, 0, pi))
    return pl.pallas_call(
        _s5_intra_kernel,
        grid=(b, gp),
        in_specs=[pl.BlockSpec((1, 1, nc, w), lambda bi, pi: (bi, pi, 0, 0)),
                  pl.BlockSpec((1, w, w), lambda bi, pi: (pi, 0, 0)),
                  pl.BlockSpec((2, 1, w, 2 * S5_STATE), lambda bi, pi: (0, pi, 0, 0)),
                  pl.BlockSpec((2, 1, w, 2 * S5_STATE), lambda bi, pi: (0, pi, 0, 0))],
        out_specs=[pl.BlockSpec((1, 1, nc, w), lambda bi, pi: (bi, pi, 0, 0)), summ, summ],
        out_shape=[jax.ShapeDtypeStruct((b, gp, nc, w), F32),
                   jax.ShapeDtypeStruct((b, 2, nc, lanes), F32),
                   jax.ShapeDtypeStruct((b, 2, nc, lanes), F32)],
        compiler_params=_cparams(("arbitrary", "arbitrary")),
        name="s5_intra",
    )(ug, tmat, bre, bim)


def _s5_scan_kernel(sfr, sfi, sbr, sbi, ar_ref, ai_ref, hfr, hfi, hbr, hbi, st):
    @pl.when(pl.program_id(0) == 0)
    def _():
        st[...] = jnp.zeros_like(st)

    kb = sfr.shape[2]
    afr, afi, abr, abi = ar_ref[0], ai_ref[0], ar_ref[1], ai_ref[1]
    fr, fi, br, bi = st[0], st[1], st[2], st[3]
    for k in range(kb):
        up = slice(k, k + 1)
        dn = slice(kb - 1 - k, kb - k)
        hfr[:, up, :] = fr
        hfi[:, up, :] = fi
        hbr[:, dn, :] = br
        hbi[:, dn, :] = bi
        fr, fi = afr * fr - afi * fi + sfr[:, 0, up, :], afr * fi + afi * fr + sfi[:, 0, up, :]
        br, bi = abr * br - abi * bi + sbr[:, 0, dn, :], abr * bi + abi * br + sbi[:, 0, dn, :]
    st[0], st[1], st[2], st[3] = fr, fi, br, bi


def _s5_scan(sr, si, ar, ai, n_lat_blocks):
    b, _, nc, lanes = sr.shape
    kb = S5_CHUNK

    def f_blk(j):
        return jnp.where(j == 0, n_lat_blocks, j - 1)

    def b_blk(j):
        return jnp.where(j == 0, n_lat_blocks, n_lat_blocks - j)

    def summ(d, blk):
        return pl.BlockSpec((b, 1, kb, lanes), lambda j: (0, d, blk(j), 0))

    def state(blk):
        return pl.BlockSpec((b, kb, lanes), lambda j: (0, blk(j), 0))

    par = pl.BlockSpec((2, 1, lanes), lambda j: (0, 0, 0))
    return pl.pallas_call(
        _s5_scan_kernel,
        grid=(nc // kb,),
        in_specs=[summ(0, f_blk), summ(0, f_blk), summ(1, b_blk), summ(1, b_blk), par, par],
        out_specs=[state(f_blk), state(f_blk), state(b_blk), state(b_blk)],
        out_shape=[jax.ShapeDtypeStruct((b, nc, lanes), F32)] * 4,
        scratch_shapes=[pltpu.VMEM((4, b, 1, lanes), F32)],
        compiler_params=_cparams(("arbitrary",)),
        name="s5_scan",
    )(sr, si, sr, si, ar, ai)


def _s5_inter_kernel(y_ref, hfr, hfi, hbr, hbi, cre_ref, cim_ref, o_ref):
    y = y_ref[0, 0]
    for d, (hr, hi) in enumerate(((hfr, hfi), (hbr, hbi))):
        y = y + jnp.dot(hr[0].astype(BF16), cre_ref[d, 0], preferred_element_type=F32)
        y = y + jnp.dot(hi[0].astype(BF16), cim_ref[d, 0], preferred_element_type=F32)
    o_ref[0, 0] = y.astype(o_ref.dtype)


def _s5_inter(y_intra, states, cre, cim):
    b, gp, nc, w = y_intra.shape
    own = pl.BlockSpec((1, 1, nc, w), lambda bi, pi: (bi, pi, 0, 0))
    st = pl.BlockSpec((1, nc, 2 * S5_STATE), lambda bi, pi: (bi, 0, pi))
    tab = pl.BlockSpec((2, 1, 2 * S5_STATE, w), lambda bi, pi: (0, pi, 0, 0))
    return pl.pallas_call(
        _s5_inter_kernel,
        grid=(b, gp),
        in_specs=[own, st, st, st, st, tab, tab],
        out_specs=own,
        out_shape=jax.ShapeDtypeStruct((b, gp, nc, w), BF16),
        compiler_params=_cparams(("arbitrary", "arbitrary")),
        name="s5_inter",
    )(y_intra, *states, cre, cim)


def _s5_mixer(u, tables, s_lat):
    tmat, bre, bim, cre, cim, (acr, aci) = tables
    b, t, w = u.shape
    c, gp, jc = S5_CHUNK, S5_GROUPS // 2, S5_GROUP_CH
    nc = t // c
    assert (t - s_lat) == c * c and s_lat % (c * c) == 0
    ug = u.reshape(b, nc, c, gp, 2, jc).transpose(0, 3, 1, 4, 2, 5).reshape(b, gp, nc, 2 * c * jc).astype(BF16)
    y_intra, sr, si = _s5_intra(ug, tmat, bre, bim)
    states = _s5_scan(sr, si, acr, aci, s_lat // (c * c))
    y = _s5_inter(y_intra, states, cre, cim)
    return y.reshape(b, gp, nc, 2, c, jc).transpose(0, 2, 4, 1, 3, 5).reshape(b, t, w)


def _merge_kernel(ya, yb, ym, ys, su, dd, wglu, bglu, gates, wb, wo, x_ref, gt1, g2, sh2, sc2, wr, eb,
                  xo_ref, hp_ref, idx_ref, gw_ref, hist_ref):
    d = x_ref.shape[-1]
    y5 = ys[0] + dd[...] * su[0]
    gl = jax.nn.gelu(y5)
    yd = gl * jax.nn.sigmoid(jnp.dot(gl, wglu[...], preferred_element_type=F32) + bglu[...])
    branches = (ya[0], yb[0], ym[0], yd.astype(BF16))
    m = None
    for i in range(N_BRANCH):
        term = gates[0, :, i * d:(i + 1) * d].astype(F32) * jnp.dot(branches[i], wb[i], preferred_element_type=F32)
        m = term if m is None else m + term
    xn = x_ref[0] + gt1[0] * jnp.dot(m.astype(BF16), wo[...], preferred_element_type=F32)
    xo_ref[0] = xn
    h2 = _rms(xn, g2[...]) * (1.0 + sc2[0]) + sh2[0]
    hp_ref[0] = _pack_bf16_pair(h2)

    scores = jax.nn.sigmoid(jnp.dot(h2, wr[...], preferred_element_type=F32, precision=HIGHEST))
    sel = scores + eb[...]
    lane = lax.broadcasted_iota(I32, sel.shape, 1).astype(F32)
    idx_acc = jnp.zeros(sel.shape, F32)
    gw_acc = jnp.zeros(sel.shape, F32)
    chosen = jnp.zeros(sel.shape, F32)
    for k in range(TOP_K):
        mx = jnp.max(sel, axis=1, keepdims=True)
        ik = jnp.min(jnp.where(sel == mx, lane, float(LANES)), axis=1, keepdims=True)
        hit = lane == ik
        gk = jnp.sum(jnp.where(hit, scores, 0.0), axis=1, keepdims=True)
        idx_acc = jnp.where(lane == k, ik, idx_acc)
        gw_acc = jnp.where(lane == k, gk, gw_acc)
        chosen = jnp.where(hit, 1.0, chosen)
        sel = jnp.where(hit, -jnp.inf, sel)
    idx_ref[0] = idx_acc.astype(I32)
    gw_ref[0] = gw_acc / jnp.sum(gw_acc, axis=1, keepdims=True) * ROUTED_SCALE
    hist_ref[0, 0] = jnp.sum(chosen, axis=0, keepdims=True)


def _merge(ya, yb, ym, ys, su, dd, wglu, bglu, gates, wb, wo, x_all, gt1, g2, sh2, sc2, wr, eb, n_lat):
    b, t, d = x_all.shape
    mod_spec = pl.BlockSpec((1, 1, d), lambda bi, i: (bi * 2 + (i >= n_lat).astype(I32), 0, 0))

    def full(a):
        return pl.BlockSpec(a.shape, lambda bi, i: (0,) * a.ndim)

    def tile(w):
        return pl.BlockSpec((1, TM, w), lambda bi, i: (bi, i, 0))

    return pl.pallas_call(
        _merge_kernel,
        grid=(b, t // TM),
        in_specs=[tile(256), tile(256), tile(256), tile(256), tile(256), full(dd), full(wglu), full(bglu),
                  tile(N_BRANCH * d), full(wb), full(wo), tile(d), mod_spec, full(g2), mod_spec, mod_spec,
                  full(wr), full(eb)],
        out_specs=[tile(d), tile(d // 2), tile(LANES), tile(LANES),
                   pl.BlockSpec((1, 1, 1, LANES), lambda bi, i: (bi, i, 0, 0))],
        out_shape=[jax.ShapeDtypeStruct((b, t, d), F32), jax.ShapeDtypeStruct((b, t, d // 2), I32),
                   jax.ShapeDtypeStruct((b, t, LANES), I32), jax.ShapeDtypeStruct((b, t, LANES), F32),
                   jax.ShapeDtypeStruct((b, t // TM, 1, LANES), F32)],
        compiler_params=_cparams(("arbitrary", "arbitrary"), 48),
        name="merge",
    )(ya, yb, ym, ys, su, dd, wglu, bglu, gates, wb, wo, x_all, gt1, g2, sh2, sc2, wr, eb)


def _route_tables(hist, n_assign):
    h = hist.reshape(-1, LANES).astype(I32)
    counts = jnp.sum(h, axis=0)
    padded = (counts + EXPERT_BLOCK - 1) // EXPERT_BLOCK * EXPERT_BLOCK
    pends = jnp.cumsum(padded)
    base = (pends - padded)[None, :] + jnp.cumsum(h, axis=0) - h
    n_blocks = -(-n_assign // EXPERT_BLOCK) + N_EXPERTS
    block_start = jnp.arange(n_blocks, dtype=I32) * EXPERT_BLOCK
    blk_e = jnp.minimum(jnp.sum((pends[None, :N_EXPERTS] <= block_start[:, None]).astype(I32), axis=1),
                        N_EXPERTS - 1)
    return base.astype(F32).reshape(hist.shape), blk_e, n_blocks


def _pos_kernel(idx_ref, base_ref, pos_ref):
    idx = idx_ref[0]
    lane = lax.broadcasted_iota(I32, idx.shape, 1)
    hits = [lane == idx[:, k:k + 1] for k in range(TOP_K)]
    chosen = jnp.zeros(idx.shape, F32)
    for hit in hits:
        chosen = jnp.where(hit, 1.0, chosen)
    row = lax.broadcasted_iota(I32, (TM, TM), 0)
    col = lax.broadcasted_iota(I32, (TM, TM), 1)
    earlier = jnp.where(row > col, 1.0, 0.0).astype(BF16)
    slot = jnp.dot(earlier, chosen.astype(BF16), preferred_element_type=F32) + base_ref[0, 0]
    pos = jnp.zeros(idx.shape, F32)
    for k, hit in enumerate(hits):
        pos = jnp.where(lane == k, jnp.sum(jnp.where(hit, slot, 0.0), axis=1, keepdims=True), pos)
    pos_ref[0] = pos.astype(I32)


def _positions(idx, base):
    b, t, _ = idx.shape
    tile = pl.BlockSpec((1, TM, LANES), lambda bi, i: (bi, i, 0))
    return pl.pallas_call(
        _pos_kernel,
        grid=(b, t // TM),
        in_specs=[tile, pl.BlockSpec((1, 1, 1, LANES), lambda bi, i: (bi, i, 0, 0))],
        out_specs=tile,
        out_shape=jax.ShapeDtypeStruct((b, t, LANES), I32),
        compiler_params=_cparams(("arbitrary", "arbitrary")),
        name="moe_pos",
    )(idx, base)


def _scatter_kernel(pos_ref, hp_ref, init_ref, xs_ref, sem):
    del init_ref
    for j in range(TOP_K * TM):
        pltpu.make_async_copy(hp_ref.at[pl.ds(j % TM, 1)], xs_ref.at[pl.ds(pos_ref[0, 0, j], 1)],
                              sem).start(priority=j % 2)
    for k in range(TOP_K):
        pltpu.make_async_copy(hp_ref, xs_ref.at[pl.ds(0, TM)], sem).wait()


def _tile_major(pos):
    nt = pos.shape[0] // TM
    return pos[:, :TOP_K].reshape(nt, TM, TOP_K).transpose(0, 2, 1).reshape(nt, 1, TOP_K * TM)


def _scatter_rows(pos3, hp, n_slots):
    n, half = hp.shape
    return pl.pallas_call(
        _scatter_kernel,
        grid=(n // TM,),
        in_specs=[pl.BlockSpec((1, 1, TOP_K * TM), lambda i: (i, 0, 0), memory_space=pltpu.SMEM),
                  pl.BlockSpec((TM, half), lambda i: (i, 0)),
                  pl.BlockSpec(memory_space=pl.ANY)],
        out_specs=pl.BlockSpec(memory_space=pl.ANY),
        out_shape=jax.ShapeDtypeStruct((n_slots, half), hp.dtype),
        scratch_shapes=[pltpu.SemaphoreType.DMA(())],
        input_output_aliases={2: 0},
        compiler_params=_cparams(("arbitrary",)),
        name="moe_scatter",
    )(pos3, hp, jnp.zeros((n_slots, half), hp.dtype))


def _swiglu_packed(p, w1, w3, w2):
    lo, hi = _unpack_bf16_pair(p)
    lo, hi = lo.astype(BF16), hi.astype(BF16)
    n = lo.shape[-1]

    def up(w):
        return (jnp.dot(lo, w[:n], preferred_element_type=F32) + jnp.dot(hi, w[n:], preferred_element_type=F32))

    a = up(w1)
    return jnp.dot((a * jax.nn.sigmoid(a) * up(w3)).astype(BF16), w2, preferred_element_type=F32)


def _ffn_kernel(be_ref, x_ref, w1_ref, w3_ref, w2_ref, y_ref):
    del be_ref
    y_ref[...] = _pack_bf16_pair(_swiglu_packed(x_ref[...], w1_ref[0], w3_ref[0], w2_ref[0]))


def _expert_ffn(blk_e, xs, w1, w3, w2):
    n_slots, half = xs.shape
    e, d, ff = w1.shape
    row = pl.BlockSpec((EXPERT_BLOCK, half), lambda i, be: (i, 0))
    return pl.pallas_call(
        _ffn_kernel,
        grid_spec=pltpu.PrefetchScalarGridSpec(
            num_scalar_prefetch=1,
            grid=(n_slots // EXPERT_BLOCK,),
            in_specs=[row,
                      pl.BlockSpec((1, d, ff), lambda i, be: (be[i], 0, 0)),
                      pl.BlockSpec((1, d, ff), lambda i, be: (be[i], 0, 0)),
                      pl.BlockSpec((1, ff, d), lambda i, be: (be[i], 0, 0))],
            out_specs=row),
        out_shape=jax.ShapeDtypeStruct((n_slots, half), I32),
        compiler_params=_cparams(("arbitrary",)),
        name="moe_ffn",
    )(blk_e, xs, w1, w3, w2)


def _combine_kernel(pos_ref, gw_ref, ys_ref, hp_ref, x_ref, gt2, ws1, ws3, ws2, xo_ref, buf, sem):
    for j in range(TOP_K * TM):
        pltpu.make_async_copy(ys_ref.at[pl.ds(pos_ref[0, 0, j], 1)], buf.at[j // TM, pl.ds(j % TM, 1)],
                              sem).start(priority=j % 2)

    shared = _swiglu_packed(hp_ref[...], ws1[...], ws3[...], ws2[...])

    for k in range(TOP_K):
        pltpu.make_async_copy(ys_ref.at[pl.ds(0, TM)], buf.at[k], sem).wait()

    gw = gw_ref[...]
    lo = hi = None
    for k in range(TOP_K):
        wk = gw[:, k:k + 1]
        lk, hk = _unpack_bf16_pair(buf[k])
        lo = wk * lk if lo is None else lo + wk * lk
        hi = wk * hk if hi is None else hi + wk * hk
    xo_ref[...] = x_ref[...] + gt2[0] * (shared + jnp.concatenate([lo, hi], axis=1))


def _combine(pos3, gw, ys, hp, x_flat, gt2, ws1, ws3, ws2, tiles_per_batch, n_lat):
    n, d = x_flat.shape
    nt = n // TM

    def full(a):
        return pl.BlockSpec(a.shape, lambda i: (0,) * a.ndim)

    def tile(w):
        return pl.BlockSpec((TM, w), lambda i: (i, 0))

    def mod_idx(i):
        return (i // tiles_per_batch) * 2 + ((i % tiles_per_batch) >= n_lat).astype(I32)

    return pl.pallas_call(
        _combine_kernel,
        grid=(nt,),
        in_specs=[pl.BlockSpec((1, 1, TOP_K * TM), lambda i: (i, 0, 0), memory_space=pltpu.SMEM),
                  tile(LANES), pl.BlockSpec(memory_space=pl.ANY), tile(d // 2), tile(d),
                  pl.BlockSpec((1, 1, d), lambda i: (mod_idx(i), 0, 0)), full(ws1), full(ws3), full(ws2)],
        out_specs=tile(d),
        out_shape=jax.ShapeDtypeStruct((n, d), F32),
        scratch_shapes=[pltpu.VMEM((TOP_K, TM, d // 2), I32), pltpu.SemaphoreType.DMA(())],
        compiler_params=_cparams(("arbitrary",), 40),
        name="moe_combine",
    )(pos3, gw, ys, hp, x_flat, gt2, ws1, ws3, ws2)


def _final_kernel(x_ref, g_ref, o_ref):
    o_ref[0] = _rms(x_ref[0], g_ref[...])


def _final_norm(x_all, g, s_lat):
    b, _, d = x_all.shape
    return pl.pallas_call(
        _final_kernel,
        grid=(b, s_lat // TM),
        in_specs=[pl.BlockSpec((1, TM, d), lambda bi, i: (bi, i, 0)), pl.BlockSpec((1, d), lambda bi, i: (0, 0))],
        out_specs=pl.BlockSpec((1, TM, d), lambda bi, i: (bi, i, 0)),
        out_shape=jax.ShapeDtypeStruct((b, s_lat, d), F32),
        compiler_params=_cparams(("arbitrary", "arbitrary")),
        name="final_norm",
    )(x_all, g.reshape(1, d))


def _rope_tables(s_lat, t):
    pos = jnp.arange(s_lat, dtype=I32)
    half = 8
    freqs = ROPE_BASE ** (-jnp.arange(half, dtype=F32) * 2.0 / (2 * half))

    def part(p):
        ang = p.astype(F32)[:, None] * freqs[None, :]
        c, s = jnp.cos(ang), jnp.sin(ang)
        return jnp.concatenate([c, c], axis=1), jnp.concatenate([-s, s], axis=1)

    cr, sr = part(pos // GRID_W)
    cc, sc = part(pos % GRID_W)
    cos = jnp.tile(jnp.concatenate([cr, cc], axis=1), (1, 8))
    sin = jnp.tile(jnp.concatenate([sr, sc], axis=1), (1, 8))
    pad = t - s_lat
    return (jnp.concatenate([cos, jnp.ones((pad, 256), F32)], axis=0),
            jnp.concatenate([sin, jnp.zeros((pad, 256), F32)], axis=0))


def _heads(x, h):
    b, t, w = x.shape
    return x.reshape(b, t, h, w // h).transpose(0, 2, 1, 3)


def _tokens(o):
    b, h, t, d = o.shape
    return o.transpose(0, 2, 1, 3).reshape(b, t, h * d)


def kernel(x, c, ctx, c_ctx, w_ada, b_ada, g_norm1, g_norm2, w_in, na_rpb, diff_lambda, g_diff, g_mla_q, g_mla_kv, w_mla_uq, w_mla_ukv, s5_lam_re, s5_lam_im, s5_log_step, s5_b_re, s5_b_im, s5_c_re, s5_c_im, s5_d, w_glu, b_glu, w_branch, w_out, w_router, e_bias, w_e1, w_e3, w_e2, w_s1, w_s3, w_s2, g_final):
    b, s_lat, d = x.shape
    lc = ctx.shape[1]
    t = s_lat + lc
    depth = w_ada.shape[0]
    n_lat = s_lat // TM
    assert lc == TM and s_lat % TM == 0 and n_lat >= 3 and b + 1 <= 8 and d % 256 == 0
    tiles_per_batch = t // TM

    x_all = jnp.concatenate([x, ctx], axis=1)
    c8 = jnp.zeros((8, d), F32).at[:b].set(c).at[b].set(c_ctx)
    ada = _ada(c8, w_ada, b_ada)
    cos_t, sin_t = _rope_tables(s_lat, t)

    hq = jnp.arange(MLA_HEADS)[:, None] * (MLA_NOPE + MLA_ROPE)
    uq_cols = jnp.concatenate([(hq + jnp.arange(MLA_NOPE)[None]).reshape(-1),
                               (hq + MLA_NOPE + jnp.arange(MLA_ROPE)[None]).reshape(-1)])
    hk = jnp.arange(MLA_HEADS)[:, None] * (MLA_NOPE + MLA_V)
    ukv_cols = jnp.concatenate([(hk + jnp.arange(MLA_NOPE)[None]).reshape(-1),
                                (hk + MLA_NOPE + jnp.arange(MLA_V)[None]).reshape(-1)])
    kpe_end = 1952

    for l in range(depth):
        mods = jnp.concatenate([ada[l, :b, None, :], jnp.broadcast_to(ada[l, b], (b, 1, 6 * d))], axis=1)
        sh1, sc1, gt1, sh2, sc2, gt2 = (mods[:, :, i * d:(i + 1) * d].reshape(2 * b, 1, d) for i in range(6))
        w_main = jnp.concatenate([w_in[l][:, :kpe_end], jnp.zeros((d, OFF_U - kpe_end), F32),
                                  w_in[l][:, kpe_end:]], axis=1).astype(BF16)
        (naq, nak, nav, dfq, dfqr, dfk, dfkr, dfv, mlqn, mlqp, mlqpr, mlkn, mlv, mlkx, s5u, gates) = _proj(
            x_all, g_norm1[l].reshape(1, d), sh1, sc1, w_main, cos_t, sin_t, g_mla_q[l].reshape(1, -1),
            g_mla_kv[l].reshape(1, -1), w_mla_uq[l][:, uq_cols].astype(BF16), w_mla_ukv[l][:, ukv_cols].astype(BF16),
            n_lat)

        ya = _na(naq, nak, nav, _na_bias(na_rpb[l]), n_lat)

        lam_init = 0.8 - 0.6 * math.exp(-0.3 * l)
        dq = jnp.concatenate([_heads(dfqr, 2 * DIFF_HEADS), _heads(dfq, 2 * DIFF_HEADS)], axis=-1)
        dq = dq.reshape(b, DIFF_HEADS, 2, t, 2 * DIFF_DK).transpose(0, 1, 3, 2, 4).reshape(b, DIFF_HEADS, t, -1)
        dk = jnp.concatenate([_heads(dfkr, 2 * DIFF_HEADS), _heads(dfk, 2 * DIFF_HEADS)], axis=-1)
        dkt = dk.transpose(0, 1, 3, 2).reshape(b, DIFF_HEADS, 4 * DIFF_DK, t)
        yb = _tokens(_global_attention(
            functools.partial(_diff_kernel, lam_init=lam_init), "diff_attn", 2, dq, dkt, _heads(dfv, DIFF_HEADS),
            (diff_lambda[l], g_diff[l].reshape(1, -1)), s_lat))

        mq = jnp.concatenate([_heads(mlqn, MLA_HEADS), _heads(mlqpr, MLA_HEADS), _heads(mlqp, MLA_HEADS)], axis=-1)
        kx = jnp.broadcast_to(mlkx[:, None, :, :2 * MLA_ROPE], (b, MLA_HEADS, t, 2 * MLA_ROPE))
        mkt = jnp.concatenate([_heads(mlkn, MLA_HEADS), kx], axis=-1).transpose(0, 1, 3, 2)
        ym = _tokens(_global_attention(_mla_kernel, "mla_attn", 1, mq, mkt, _heads(mlv, MLA_HEADS), (), s_lat))

        ys = _s5_mixer(s5u, _s5_tables(s5_lam_re[l], s5_lam_im[l], s5_log_step[l], s5_b_re[l], s5_b_im[l],
                                       s5_c_re[l], s5_c_im[l]), s_lat)

        wr = jnp.concatenate([w_router[l], jnp.zeros((d, LANES - N_EXPERTS), F32)], axis=1)
        eb = jnp.concatenate([e_bias[l].astype(F32), jnp.full((LANES - N_EXPERTS,), -jnp.inf, F32)]).reshape(1, LANES)
        x_all, hp, idx, gw, hist = _merge(
            ya, yb, ym, ys, s5u, s5_d[l].reshape(1, -1), w_glu[l], b_glu[l].reshape(1, -1), gates,
            w_branch[l].astype(BF16), w_out[l].astype(BF16), x_all, gt1, g_norm2[l].reshape(1, d), sh2, sc2, wr, eb,
            n_lat)

        n = b * t
        base, blk_e, n_blocks = _route_tables(hist, n * TOP_K)
        pos3 = _tile_major(_positions(idx, base).reshape(n, LANES))
        hp_flat = hp.reshape(n, d // 2)
        xs = _scatter_rows(pos3, hp_flat, n_blocks * EXPERT_BLOCK)
        y_sorted = _expert_ffn(blk_e, xs, w_e1[l].astype(BF16), w_e3[l].astype(BF16), w_e2[l].astype(BF16))
        x_all = _combine(pos3, gw.reshape(n, LANES), y_sorted, hp_flat, x_all.reshape(n, d), gt2,
                         w_s1[l].astype(BF16), w_s3[l].astype(BF16), w_s2[l].astype(BF16),
                         tiles_per_batch, n_lat).reshape(b, t, d)

    return _final_norm(x_all, g_final, s_lat)
```

```python
import functools
import math

import jax
import jax.numpy as jnp
import numpy as np
from jax import lax
from jax.experimental import pallas as pl
from jax.experimental.pallas import tpu as pltpu

F32 = jnp.float32
BF16 = jnp.bfloat16
I32 = jnp.int32
HIGHEST = lax.Precision.HIGHEST

GRID_W = 64
HEAD_DIM = 64
NA_HEADS = 4
WIN_H = 8
WIN_W = 16
DIFF_HEADS = 4
DIFF_DK = 32
DIFF_DV = 64
MLA_HEADS = 4
MLA_NOPE = 64
MLA_ROPE = 32
MLA_V = 64
S5_GROUPS = 16
S5_GROUP_CH = 16
S5_STATE = 64
S5_WIDTH = S5_GROUPS * S5_GROUP_CH
S5_CHUNK = 16
N_BRANCH = 4
N_EXPERTS = 64
TOP_K = 6
EXPERT_BLOCK = 256
ROUTED_SCALE = 1.0
ROPE_BASE = 10000.0
EPS = 1e-6
NEG_INF = -1e30

TM = 256
ATTN_TQ = 512
LANES = 128
HI_MASK = -65536

NA_SCALE = HEAD_DIM ** -0.5
DIFF_SCALE = DIFF_DK ** -0.5
MLA_SCALE = (MLA_NOPE + MLA_ROPE) ** -0.5

OFF_NA, OFF_DQ, OFF_DK, OFF_DV, OFF_CQ, OFF_CKV, OFF_KPE, OFF_U, OFF_GATE = (
    0, 768, 1024, 1280, 1536, 1792, 1920, 2048, 2304)


def _cparams(sem, vmem_mb=None):
    kw = dict(dimension_semantics=sem)
    if vmem_mb is not None:
        kw["vmem_limit_bytes"] = vmem_mb * 2 ** 20
    return pltpu.CompilerParams(**kw)


def _rms(x, g):
    return x * lax.rsqrt(jnp.mean(x * x, axis=-1, keepdims=True) + EPS) * g


def _rope(x, cos, sin_signed):
    w = x.shape[-1]
    lane = lax.broadcasted_iota(I32, x.shape, 1)
    partner = jnp.where((lane & 8) == 0, pltpu.roll(x, w - 8, 1), pltpu.roll(x, 8, 1))
    return x * cos + partner * sin_signed


def _pack_bf16_pair(x):
    n = x.shape[-1] // 2
    bits = lax.bitcast_convert_type(x.astype(BF16).astype(F32), I32)
    return lax.shift_right_logical(bits[:, :n], 16) | (bits[:, n:] & HI_MASK)


def _unpack_bf16_pair(p):
    lo = lax.bitcast_convert_type(lax.shift_left(p, 16), F32)
    hi = lax.bitcast_convert_type(p & HI_MASK, F32)
    return lo, hi


def _ada_kernel(c_ref, w_ref, b_ref, o_ref):
    c = c_ref[...]
    s = c * jax.nn.sigmoid(c)
    o_ref[0] = jnp.dot(s, w_ref[0], preferred_element_type=F32, precision=HIGHEST) + b_ref[0]


def _ada(c8, w_ada, b_ada):
    depth, d, n = w_ada.shape
    tn = 1536
    return pl.pallas_call(
        _ada_kernel,
        grid=(depth, n // tn),
        in_specs=[pl.BlockSpec((8, d), lambda l, j: (0, 0)),
                  pl.BlockSpec((1, d, tn), lambda l, j: (l, 0, j)),
                  pl.BlockSpec((1, 1, tn), lambda l, j: (l, 0, j))],
        out_specs=pl.BlockSpec((1, 8, tn), lambda l, j: (l, 0, j)),
        out_shape=jax.ShapeDtypeStruct((depth, 8, n), F32),
        compiler_params=_cparams(("arbitrary", "arbitrary"), 40),
        name="ada",
    )(c8, w_ada, b_ada.reshape(depth, 1, n))


def _proj_kernel(n_lat, x_ref, g_ref, sh_ref, sc_ref, w_ref, cos_ref, sin_ref, gq_ref, gkv_ref, wuq_ref, wukv_ref,
                 naq, nak, nav, dfq, dfqr, dfk, dfkr, dfv, mlqn, mlqp, mlqpr, mlkn, mlv, mlkx, s5u, gates):
    is_ctx = pl.program_id(1) >= n_lat
    m_lat = jnp.where(is_ctx, 0.0, 1.0).astype(F32)
    m_ctx = 1.0 - m_lat
    h = (_rms(x_ref[0], g_ref[...]) * (1.0 + sc_ref[0]) + sh_ref[0]).astype(BF16)

    def proj(off, width):
        return jnp.dot(h, w_ref[:, off:off + width], preferred_element_type=F32)

    cos = cos_ref[...]
    sin = sin_ref[...]
    a = proj(OFF_NA, 768)
    naq[0] = (a[:, :256] * NA_SCALE).astype(BF16)
    nak[0] = a[:, 256:512].astype(BF16)
    nav[0] = a[:, 512:].astype(BF16)

    bq = proj(OFF_DQ, 256) * DIFF_SCALE
    dfq[0] = bq.astype(BF16)
    dfqr[0] = _rope(bq, cos, sin).astype(BF16)
    bk = proj(OFF_DK, 256)
    dfk[0] = (bk * m_ctx).astype(BF16)
    dfkr[0] = (_rope(bk, cos, sin) * m_lat).astype(BF16)
    dfv[0] = proj(OFF_DV, 256).astype(BF16)

    cq = _rms(proj(OFF_CQ, 256), gq_ref[...]).astype(BF16)
    q2 = jnp.dot(cq, wuq_ref[...], preferred_element_type=F32) * MLA_SCALE
    qp = q2[:, 256:]
    mlqn[0] = q2[:, :256].astype(BF16)
    mlqp[0] = qp.astype(BF16)
    mlqpr[0] = _rope(qp, cos[:, :LANES], sin[:, :LANES]).astype(BF16)
    ckv = _rms(proj(OFF_CKV, 128), gkv_ref[...]).astype(BF16)
    kv = jnp.dot(ckv, wukv_ref[...], preferred_element_type=F32)
    mlkn[0] = kv[:, :256].astype(BF16)
    mlv[0] = kv[:, 256:].astype(BF16)
    kpe = proj(OFF_KPE, LANES)
    kper = _rope(kpe, cos[:, :LANES], sin[:, :LANES])
    mlkx[0] = (kper * m_lat + pltpu.roll(kpe, MLA_ROPE, 1) * m_ctx).astype(BF16)

    s5u[0] = proj(OFF_U, 256)
    d = x_ref.shape[-1]
    for i in range(N_BRANCH):
        gates[0, :, i * d:(i + 1) * d] = jax.nn.sigmoid(proj(OFF_GATE + i * d, d)).astype(BF16)


def _proj(x_all, g1, shift, scale, w_main, cos_t, sin_t, g_q, g_kv, w_uq, w_ukv, n_lat):
    b, t, d = x_all.shape
    nt = t // TM
    mod_spec = pl.BlockSpec((1, 1, d), lambda bi, i: (bi * 2 + (i >= n_lat).astype(I32), 0, 0))

    def full(a):
        return pl.BlockSpec(a.shape, lambda bi, i: (0,) * a.ndim)

    def tile(w):
        return pl.BlockSpec((1, TM, w), lambda bi, i: (bi, i, 0))

    widths = [256, 256, 256, 256, 256, 256, 256, 256, 256, 128, 128, 256, 256, 128, 256, N_BRANCH * d]
    dtypes = [BF16] * 14 + [F32, BF16]
    return pl.pallas_call(
        functools.partial(_proj_kernel, n_lat),
        grid=(b, nt),
        in_specs=[tile(d), full(g1), mod_spec, mod_spec, full(w_main),
                  pl.BlockSpec((TM, 256), lambda bi, i: (i, 0)), pl.BlockSpec((TM, 256), lambda bi, i: (i, 0)),
                  full(g_q), full(g_kv), full(w_uq), full(w_ukv)],
        out_specs=[tile(w) for w in widths],
        out_shape=[jax.ShapeDtypeStruct((b, t, w), dt) for w, dt in zip(widths, dtypes)],
        compiler_params=_cparams(("arbitrary", "arbitrary"), 56),
        name="proj",
    )(x_all, g1, shift, scale, w_main, cos_t, sin_t, g_q, g_kv, w_uq, w_ukv)


def _na_kernel(q_ref, k0, k1, k2, kc, v0, v1, v2, vc, bias_ref, o_ref):
    nt_dims = (((1,), (1,)), ((), ()))
    outs = []
    for h in range(NA_HEADS):
        sl = slice(h * HEAD_DIM, (h + 1) * HEAD_DIM)
        qh = q_ref[0, :, sl]
        sw = jnp.concatenate(
            [lax.dot_general(qh, kr[0, :, sl], nt_dims, preferred_element_type=F32) for kr in (k0, k1, k2)],
            axis=1) + bias_ref[0, h]
        sc = lax.dot_general(qh, kc[0, :, sl], nt_dims, preferred_element_type=F32)
        m = jnp.maximum(jnp.max(sw, axis=1, keepdims=True), jnp.max(sc, axis=1, keepdims=True))
        pw = jnp.exp(sw - m)
        pc = jnp.exp(sc - m)
        l = jnp.sum(pw, axis=1, keepdims=True) + jnp.sum(pc, axis=1, keepdims=True)
        o = jnp.dot(pc.astype(BF16), vc[0, :, sl], preferred_element_type=F32)
        for j, vr in enumerate((v0, v1, v2)):
            o = o + jnp.dot(pw[:, j * TM:(j + 1) * TM].astype(BF16), vr[0, :, sl], preferred_element_type=F32)
        outs.append(o / l)
    o_ref[0] = jnp.concatenate(outs, axis=1).astype(BF16)


def _na_bias(rpb):
    rows_q = TM // GRID_W
    rows_k = 3 * rows_q
    qr = np.arange(rows_q)
    kr = np.arange(rows_k)
    col = np.arange(GRID_W)
    cstart = np.clip(col - WIN_W // 2, 0, GRID_W - WIN_W)
    col_ok = (col[None, :] >= cstart[:, None]) & (col[None, :] < cstart[:, None] + WIN_W)
    dc = np.clip(col[None, :] - col[:, None], 1 - WIN_W, WIN_W - 1) + (WIN_W - 1)
    sel_c = (dc[:, :, None] == np.arange(2 * WIN_W - 1)).astype(np.float32)
    kinds = []
    for start_rel, q_rel in ((np.zeros_like(qr), qr), (qr, qr + 4), (np.full_like(qr, 4), qr + 8)):
        row_ok = (kr[None, :] >= start_rel[:, None]) & (kr[None, :] < start_rel[:, None] + WIN_H)
        dr = np.clip(kr[None, :] - q_rel[:, None] + (WIN_H - 1), 0, 2 * WIN_H - 2)
        sel_r = (dr[:, :, None] == np.arange(2 * WIN_H - 1)).astype(np.float32)
        bias = jnp.einsum('hab,rsa,cdb->hrcsd', rpb.astype(F32), sel_r, sel_c, precision=HIGHEST)
        ok = row_ok[:, None, :, None] & col_ok[None, :, None, :]
        kinds.append(jnp.where(ok[None], bias, NEG_INF).reshape(NA_HEADS, TM, 3 * TM))
    kinds.append(jnp.full((NA_HEADS, TM, 3 * TM), NEG_INF, F32))
    return jnp.stack(kinds)


def _na(q, k, v, bias, n_lat):
    b, t, w = q.shape
    nt = t // TM

    def base(i):
        return jnp.clip(i - 1, 0, n_lat - 3)

    def kind(i):
        return jnp.where(i >= n_lat, 3, jnp.where(i == 0, 0, jnp.where(i == n_lat - 1, 2, 1)))

    def win(j):
        return pl.BlockSpec((1, TM, w), lambda bi, i: (bi, base(i) + j, 0))

    ctx = pl.BlockSpec((1, TM, w), lambda bi, i: (bi, n_lat, 0))
    own = pl.BlockSpec((1, TM, w), lambda bi, i: (bi, i, 0))
    return pl.pallas_call(
        _na_kernel,
        grid=(b, nt),
        in_specs=[own, win(0), win(1), win(2), ctx, win(0), win(1), win(2), ctx,
                  pl.BlockSpec((1, NA_HEADS, TM, 3 * TM), lambda bi, i: (kind(i), 0, 0, 0))],
        out_specs=own,
        out_shape=jax.ShapeDtypeStruct((b, t, w), BF16),
        compiler_params=_cparams(("arbitrary", "arbitrary"), 40),
        name="na_attn",
    )(q, k, k, k, k, v, v, v, v, bias)


def _attend(n_maps, s_lat, tk, q_ref, kt_ref, v_ref, s_ref, emit):
    tq = q_ref.shape[2]
    t = v_ref.shape[2]
    dv = v_ref.shape[3] // 2
    dk = q_ref.shape[3] // n_maps
    maps = range(n_maps)
    qs = [q_ref[0, 0, :, m * dk:(m + 1) * dk] for m in maps]
    init = tuple((jnp.full((tq, 1), NEG_INF, F32), jnp.zeros((tq, 2 * dv), F32)) for _ in maps)

    def scores(slot, j):
        off = j * tk
        mx = []
        for m in maps:
            s = jnp.dot(qs[m], kt_ref[0, 0, m * dk:(m + 1) * dk, pl.ds(off, tk)], preferred_element_type=F32)
            s_ref[m, slot] = s
            mx.append(jnp.max(s, axis=1, keepdims=True))
        return tuple(mx)

    def update(s, mx, v1, carry):
        m_old, acc = carry
        m_new = jnp.maximum(m_old, mx)
        p = jnp.exp(s - m_new).astype(BF16)
        return m_new, jnp.exp(m_old - m_new) * acc + jnp.dot(p, v1, preferred_element_type=F32)

    def absorb(slot, j, mx, carry):
        v1 = v_ref[0, 0, pl.ds(j * tk, tk), :]
        return tuple(update(s_ref[m, slot], mx[m], v1, carry[m]) for m in maps)

    def ctx_scores():
        return [jnp.dot(qs[m], kt_ref[0, 0, m * dk:(m + 1) * dk, s_lat:t], preferred_element_type=F32) for m in maps]

    def finish(sc, carry):
        vc = v_ref[0, 0, s_lat:t, :]
        outs = []
        for m in maps:
            _, acc = update(sc[m], jnp.max(sc[m], axis=1, keepdims=True), vc, carry[m])
            outs.append(acc[:, :dv] / acc[:, dv:dv + 1])
        emit(outs)

    if s_ref is None:
        finish(ctx_scores(), init)
        return

    n_steps = s_lat // tk
    carry, mx = init, scores(0, 0)
    for j in range(n_steps - 1):
        mx_next = scores((j + 1) % 2, j + 1)
        carry = absorb(j % 2, j, mx, carry)
        mx = mx_next
    sc = ctx_scores()
    finish(sc, absorb((n_steps - 1) % 2, n_steps - 1, mx, carry))


def _mla_kernel(s_lat, tk, q_ref, kt_ref, v_ref, o_ref, s_ref=None):
    def emit(outs):
        o_ref[0, 0] = outs[0].astype(BF16)

    _attend(1, s_lat, tk, q_ref, kt_ref, v_ref, s_ref, emit)


def _diff_kernel(s_lat, tk, q_ref, kt_ref, v_ref, lam_ref, g_ref, o_ref, s_ref=None, *, lam_init):
    def emit(outs):
        lam = lam_ref[...]
        lam_full = (jnp.exp(jnp.sum(lam[0:1] * lam[1:2], axis=1, keepdims=True))
                    - jnp.exp(jnp.sum(lam[2:3] * lam[3:4], axis=1, keepdims=True)) + lam_init)
        d = _rms(outs[0] - lam_full * outs[1], g_ref[...]) * (1.0 - lam_init)
        o_ref[0, 0] = d.astype(BF16)

    _attend(2, s_lat, tk, q_ref, kt_ref, v_ref, s_ref, emit)


def _with_ones(v):
    pad = jnp.zeros(v.shape[:-1] + (v.shape[-1] - 1,), v.dtype)
    return jnp.concatenate([v, jnp.ones(v.shape[:-1] + (1,), v.dtype), pad], axis=-1)


def _global_attention(kernel, name, n_maps, q, kt, v, extra, s_lat):
    b, h, t, dq = q.shape
    dv = v.shape[-1]
    dk = kt.shape[2]
    v = _with_ones(v)
    tk = min(1024, s_lat)
    tq = ATTN_TQ if s_lat % ATTN_TQ == 0 else TM
    lc = t - s_lat
    c_blk = s_lat // lc
    params = _cparams(("arbitrary", "arbitrary", "arbitrary"), 48)
    extra_specs = [pl.BlockSpec(a.shape, lambda bi, hi, i: (0, 0)) for a in extra]
    lat = pl.pallas_call(
        functools.partial(kernel, s_lat, tk),
        grid=(b, h, s_lat // tq),
        in_specs=[pl.BlockSpec((1, 1, tq, dq), lambda bi, hi, i: (bi, hi, i, 0)),
                  pl.BlockSpec((1, 1, dk, t), lambda bi, hi, i: (bi, hi, 0, 0)),
                  pl.BlockSpec((1, 1, t, 2 * dv), lambda bi, hi, i: (bi, hi, 0, 0))] + extra_specs,
        out_specs=pl.BlockSpec((1, 1, tq, dv), lambda bi, hi, i: (bi, hi, i, 0)),
        out_shape=jax.ShapeDtypeStruct((b, h, s_lat, dv), BF16),
        scratch_shapes=[pltpu.VMEM((n_maps, 2, tq, tk), F32)],
        compiler_params=params,
        name=name,
    )(q, kt, v, *extra)
    ctx = pl.pallas_call(
        functools.partial(kernel, 0, tk),
        grid=(b, h, 1),
        in_specs=[pl.BlockSpec((1, 1, lc, dq), lambda bi, hi, i: (bi, hi, c_blk, 0)),
                  pl.BlockSpec((1, 1, dk, lc), lambda bi, hi, i: (bi, hi, 0, c_blk)),
                  pl.BlockSpec((1, 1, lc, 2 * dv), lambda bi, hi, i: (bi, hi, c_blk, 0))] + extra_specs,
        out_specs=pl.BlockSpec((1, 1, lc, dv), lambda bi, hi, i: (bi, hi, 0, 0)),
        out_shape=jax.ShapeDtypeStruct((b, h, lc, dv), BF16),
        compiler_params=params,
        name=name + "_ctx",
    )(q, kt, v, *extra)
    return jnp.concatenate([lat, ctx], axis=2)


def _s5_tables(lam_re, lam_im, log_step, b_re, b_im, c_re, c_im):
    c = S5_CHUNK
    lr, li = lam_re.astype(F32), lam_im.astype(F32)
    step = jnp.exp(log_step.astype(F32))[..., None]
    mag = jnp.exp(lr * step)
    ar, ai = mag * jnp.cos(li * step), mag * jnp.sin(li * step)
    den = lr * lr + li * li
    fr = ((ar - 1.0) * lr + ai * li) / den
    fi = (ai * lr - (ar - 1.0) * li) / den
    br, bi = b_re.astype(F32), b_im.astype(F32)
    bbr = fr[..., None] * br - fi[..., None] * bi
    bbi = fr[..., None] * bi + fi[..., None] * br
    tau = jnp.arange(c + 1, dtype=F32)[:, None, None, None]
    pmag = jnp.exp(lr * step * tau)
    pr, pi = pmag * jnp.cos(li * step * tau), pmag * jnp.sin(li * step * tau)
    cr, ci = c_re.astype(F32), c_im.astype(F32)
    abr = pr[..., None] * bbr - pi[..., None] * bbi
    abi = pr[..., None] * bbi + pi[..., None] * bbr
    kk = jnp.einsum('dgip,tdgpj->tdgij', cr, abr) - jnp.einsum('dgip,tdgpj->tdgij', ci, abi)
    s_idx = jnp.arange(c)[:, None]
    t_idx = jnp.arange(c)[None, :]
    lag_f = jnp.clip(t_idx - s_idx, 0, c)
    lag_b = jnp.clip(s_idx - t_idx, 0, c)
    tf = jnp.where((s_idx <= t_idx)[:, :, None, None, None], kk[lag_f, 0], 0.0)
    tb = jnp.where((s_idx >= t_idx)[:, :, None, None, None], kk[lag_b, 1], 0.0)
    g, j_ch = S5_GROUPS, S5_GROUP_CH
    tmat = (tf + tb).transpose(2, 0, 4, 1, 3).reshape(g, c * j_ch, c * j_ch)
    pow_f = jnp.arange(c - 1, -1, -1)
    pow_b = jnp.arange(c)

    def exit_map(d, pw):
        re = abr[pw, d].transpose(1, 0, 3, 2).reshape(g, c * j_ch, S5_STATE)
        im = abi[pw, d].transpose(1, 0, 3, 2).reshape(g, c * j_ch, S5_STATE)
        return jnp.concatenate([re, im], axis=-1)

    bsum = jnp.stack([exit_map(0, pow_f), exit_map(1, pow_b)])

    def entry_map(d, pw):
        p_r, p_i = pr[pw, d], pi[pw, d]
        on_re = jnp.einsum('gip,tgp->gpti', cr[d], p_r) - jnp.einsum('gip,tgp->gpti', ci[d], p_i)
        on_im = -jnp.einsum('gip,tgp->gpti', cr[d], p_i) - jnp.einsum('gip,tgp->gpti', ci[d], p_r)
        return jnp.concatenate([on_re, on_im], axis=1).reshape(g, 2 * S5_STATE, c * j_ch)

    cin = jnp.stack([entry_map(0, jnp.arange(1, c + 1)), entry_map(1, jnp.arange(c, 0, -1))])

    def pair_diag(a):
        a = a.reshape(a.shape[:-3] + (g // 2, 2) + a.shape[-2:])
        z = jnp.zeros_like(a[..., 0, :, :])
        return jnp.concatenate([jnp.concatenate([a[..., 0, :, :], z], axis=-1),
                                jnp.concatenate([z, a[..., 1, :, :]], axis=-1)], axis=-2).astype(BF16)

    a_c = tuple(a.reshape(2, 1, g * S5_STATE) for a in (pr[c], pi[c]))
    return (pair_diag(tmat), pair_diag(bsum[..., :S5_STATE]), pair_diag(bsum[..., S5_STATE:]),
            pair_diag(cin[..., :S5_STATE, :]), pair_diag(cin[..., S5_STATE:, :]), a_c)


def _s5_intra_kernel(u_ref, tm_ref, bre_ref, bim_ref, y_ref, sr_ref, si_ref):
    u = u_ref[0, 0]
    y_ref[0, 0] = jnp.dot(u, tm_ref[0], preferred_element_type=F32)
    for d in range(2):
        sr_ref[0, d] = jnp.dot(u, bre_ref[d, 0], preferred_element_type=F32)
        si_ref[0, d] = jnp.dot(u, bim_ref[d, 0], preferred_element_type=F32)


def _s5_intra(ug, tmat, bre, bim):
    b, gp, nc, w = ug.shape
    lanes = S5_GROUPS * S5_STATE
    summ = pl.BlockSpec((1, 2, nc, 2 * S5_STATE), lambda bi, pi: (bi, 0, 0, pi))
    return pl.pallas_call(
        _s5_intra_kernel,
        grid=(b, gp),
        in_specs=[pl.BlockSpec((1, 1, nc, w), lambda bi, pi: (bi, pi, 0, 0)),
                  pl.BlockSpec((1, w, w), lambda bi, pi: (pi, 0, 0)),
                  pl.BlockSpec((2, 1, w, 2 * S5_STATE), lambda bi, pi: (0, pi, 0, 0)),
                  pl.BlockSpec((2, 1, w, 2 * S5_STATE), lambda bi, pi: (0, pi, 0, 0))],
        out_specs=[pl.BlockSpec((1, 1, nc, w), lambda bi, pi: (bi, pi, 0, 0)), summ, summ],
        out_shape=[jax.ShapeDtypeStruct((b, gp, nc, w), F32),
                   jax.ShapeDtypeStruct((b, 2, nc, lanes), F32),
                   jax.ShapeDtypeStruct((b, 2, nc, lanes), F32)],
        compiler_params=_cparams(("arbitrary", "arbitrary")),
        name="s5_intra",
    )(ug, tmat, bre, bim)


def _s5_scan_kernel(sfr, sfi, sbr, sbi, ar_ref, ai_ref, hfr, hfi, hbr, hbi, st):
    @pl.when(pl.program_id(0) == 0)
    def _():
        st[...] = jnp.zeros_like(st)

    kb = sfr.shape[2]
    afr, afi, abr, abi = ar_ref[0], ai_ref[0], ar_ref[1], ai_ref[1]
    fr, fi, br, bi = st[0], st[1], st[2], st[3]
    for k in range(kb):
        up = slice(k, k + 1)
        dn = slice(kb - 1 - k, kb - k)
        hfr[:, up, :] = fr
        hfi[:, up, :] = fi
        hbr[:, dn, :] = br
        hbi[:, dn, :] = bi
        fr, fi = afr * fr - afi * fi + sfr[:, 0, up, :], afr * fi + afi * fr + sfi[:, 0, up, :]
        br, bi = abr * br - abi * bi + sbr[:, 0, dn, :], abr * bi + abi * br + sbi[:, 0, dn, :]
    st[0], st[1], st[2], st[3] = fr, fi, br, bi


def _s5_scan(sr, si, ar, ai, n_lat_blocks):
    b, _, nc, lanes = sr.shape
    kb = S5_CHUNK

    def f_blk(j):
        return jnp.where(j == 0, n_lat_blocks, j - 1)

    def b_blk(j):
        return jnp.where(j == 0, n_lat_blocks, n_lat_blocks - j)

    def summ(d, blk):
        return pl.BlockSpec((b, 1, kb, lanes), lambda j: (0, d, blk(j), 0))

    def state(blk):
        return pl.BlockSpec((b, kb, lanes), lambda j: (0, blk(j), 0))

    par = pl.BlockSpec((2, 1, lanes), lambda j: (0, 0, 0))
    return pl.pallas_call(
        _s5_scan_kernel,
        grid=(nc // kb,),
        in_specs=[summ(0, f_blk), summ(0, f_blk), summ(1, b_blk), summ(1, b_blk), par, par],
        out_specs=[state(f_blk), state(f_blk), state(b_blk), state(b_blk)],
        out_shape=[jax.ShapeDtypeStruct((b, nc, lanes), F32)] * 4,
        scratch_shapes=[pltpu.VMEM((4, b, 1, lanes), F32)],
        compiler_params=_cparams(("arbitrary",)),
        name="s5_scan",
    )(sr, si, sr, si, ar, ai)


def _s5_inter_kernel(y_ref, hfr, hfi, hbr, hbi, cre_ref, cim_ref, o_ref):
    y = y_ref[0, 0]
    for d, (hr, hi) in enumerate(((hfr, hfi), (hbr, hbi))):
        y = y + jnp.dot(hr[0].astype(BF16), cre_ref[d, 0], preferred_element_type=F32)
        y = y + jnp.dot(hi[0].astype(BF16), cim_ref[d, 0], preferred_element_type=F32)
    o_ref[0, 0] = y.astype(o_ref.dtype)


def _s5_inter(y_intra, states, cre, cim):
    b, gp, nc, w = y_intra.shape
    own = pl.BlockSpec((1, 1, nc, w), lambda bi, pi: (bi, pi, 0, 0))
    st = pl.BlockSpec((1, nc, 2 * S5_STATE), lambda bi, pi: (bi, 0, pi))
    tab = pl.BlockSpec((2, 1, 2 * S5_STATE, w), lambda bi, pi: (0, pi, 0, 0))
    return pl.pallas_call(
        _s5_inter_kernel,
        grid=(b, gp),
        in_specs=[own, st, st, st, st, tab, tab],
        out_specs=own,
        out_shape=jax.ShapeDtypeStruct((b, gp, nc, w), BF16),
        compiler_params=_cparams(("arbitrary", "arbitrary")),
        name="s5_inter",
    )(y_intra, *states, cre, cim)


def _s5_mixer(u, tables, s_lat):
    tmat, bre, bim, cre, cim, (acr, aci) = tables
    b, t, w = u.shape
    c, gp, jc = S5_CHUNK, S5_GROUPS // 2, S5_GROUP_CH
    nc = t // c
    assert (t - s_lat) == c * c and s_lat % (c * c) == 0
    ug = u.reshape(b, nc, c, gp, 2, jc).transpose(0, 3, 1, 4, 2, 5).reshape(b, gp, nc, 2 * c * jc).astype(BF16)
    y_intra, sr, si = _s5_intra(ug, tmat, bre, bim)
    states = _s5_scan(sr, si, acr, aci, s_lat // (c * c))
    y = _s5_inter(y_intra, states, cre, cim)
    return y.reshape(b, gp, nc, 2, c, jc).transpose(0, 2, 4, 1, 3, 5).reshape(b, t, w)


def _merge_kernel(ya, yb, ym, ys, su, dd, wglu, bglu, gates, wb, wo, x_ref, gt1, g2, sh2, sc2, wr, eb,
                  xo_ref, hp_ref, idx_ref, gw_ref, hist_ref):
    d = x_ref.shape[-1]
    y5 = ys[0] + dd[...] * su[0]
    gl = jax.nn.gelu(y5)
    yd = gl * jax.nn.sigmoid(jnp.dot(gl, wglu[...], preferred_element_type=F32) + bglu[...])
    branches = (ya[0], yb[0], ym[0], yd.astype(BF16))
    m = None
    for i in range(N_BRANCH):
        term = gates[0, :, i * d:(i + 1) * d].astype(F32) * jnp.dot(branches[i], wb[i], preferred_element_type=F32)
        m = term if m is None else m + term
    xn = x_ref[0] + gt1[0] * jnp.dot(m.astype(BF16), wo[...], preferred_element_type=F32)
    xo_ref[0] = xn
    h2 = _rms(xn, g2[...]) * (1.0 + sc2[0]) + sh2[0]
    hp_ref[0] = _pack_bf16_pair(h2)

    scores = jax.nn.sigmoid(jnp.dot(h2, wr[...], preferred_element_type=F32, precision=HIGHEST))
    sel = scores + eb[...]
    lane = lax.broadcasted_iota(I32, sel.shape, 1).astype(F32)
    idx_acc = jnp.zeros(sel.shape, F32)
    gw_acc = jnp.zeros(sel.shape, F32)
    chosen = jnp.zeros(sel.shape, F32)
    hits = []
    for k in range(TOP_K):
        mx = jnp.max(sel, axis=1, keepdims=True)
        ik = jnp.min(jnp.where(sel == mx, lane, float(LANES)), axis=1, keepdims=True)
        hit = lane == ik
        hits.append(hit)
        idx_acc = jnp.where(lane == k, ik, idx_acc)
        chosen = jnp.where(hit, 1.0, chosen)
        sel = jnp.where(hit, -jnp.inf, sel)
    for k, hit in enumerate(hits):
        gk = jnp.sum(jnp.where(hit, scores, 0.0), axis=1, keepdims=True)
        gw_acc = jnp.where(lane == k, gk, gw_acc)
    idx_ref[0] = idx_acc.astype(I32)
    gw_ref[0] = gw_acc / jnp.sum(gw_acc, axis=1, keepdims=True) * ROUTED_SCALE
    hist_ref[0, 0] = jnp.sum(chosen, axis=0, keepdims=True)


def _merge(ya, yb, ym, ys, su, dd, wglu, bglu, gates, wb, wo, x_all, gt1, g2, sh2, sc2, wr, eb, n_lat):
    b, t, d = x_all.shape
    mod_spec = pl.BlockSpec((1, 1, d), lambda bi, i: (bi * 2 + (i >= n_lat).astype(I32), 0, 0))

    def full(a):
        return pl.BlockSpec(a.shape, lambda bi, i: (0,) * a.ndim)

    def tile(w):
        return pl.BlockSpec((1, TM, w), lambda bi, i: (bi, i, 0))

    return pl.pallas_call(
        _merge_kernel,
        grid=(b, t // TM),
        in_specs=[tile(256), tile(256), tile(256), tile(256), tile(256), full(dd), full(wglu), full(bglu),
                  tile(N_BRANCH * d), full(wb), full(wo), tile(d), mod_spec, full(g2), mod_spec, mod_spec,
                  full(wr), full(eb)],
        out_specs=[tile(d), tile(d // 2), tile(LANES), tile(LANES),
                   pl.BlockSpec((1, 1, 1, LANES), lambda bi, i: (bi, i, 0, 0))],
        out_shape=[jax.ShapeDtypeStruct((b, t, d), F32), jax.ShapeDtypeStruct((b, t, d // 2), I32),
                   jax.ShapeDtypeStruct((b, t, LANES), I32), jax.ShapeDtypeStruct((b, t, LANES), F32),
                   jax.ShapeDtypeStruct((b, t // TM, 1, LANES), F32)],
        compiler_params=_cparams(("arbitrary", "arbitrary"), 48),
        name="merge",
    )(ya, yb, ym, ys, su, dd, wglu, bglu, gates, wb, wo, x_all, gt1, g2, sh2, sc2, wr, eb)


def _route_tables(hist, n_assign):
    h = hist.reshape(-1, LANES).astype(I32)
    counts = jnp.sum(h, axis=0)
    padded = (counts + EXPERT_BLOCK - 1) // EXPERT_BLOCK * EXPERT_BLOCK
    pends = jnp.cumsum(padded)
    base = (pends - padded)[None, :] + jnp.cumsum(h, axis=0) - h
    n_blocks = -(-n_assign // EXPERT_BLOCK) + N_EXPERTS
    block_start = jnp.arange(n_blocks, dtype=I32) * EXPERT_BLOCK
    blk_e = jnp.minimum(jnp.sum((pends[None, :N_EXPERTS] <= block_start[:, None]).astype(I32), axis=1),
                        N_EXPERTS - 1)
    return base.astype(F32).reshape(hist.shape), blk_e, n_blocks


def _pos_kernel(idx_ref, base_ref, pos_ref):
    idx = idx_ref[0]
    lane = lax.broadcasted_iota(I32, idx.shape, 1)
    hits = [lane == idx[:, k:k + 1] for k in range(TOP_K)]
    chosen = jnp.zeros(idx.shape, F32)
    for hit in hits:
        chosen = jnp.where(hit, 1.0, chosen)
    row = lax.broadcasted_iota(I32, (TM, TM), 0)
    col = lax.broadcasted_iota(I32, (TM, TM), 1)
    earlier = jnp.where(row > col, 1.0, 0.0).astype(BF16)
    slot = jnp.dot(earlier, chosen.astype(BF16), preferred_element_type=F32) + base_ref[0, 0]
    pos = jnp.zeros(idx.shape, F32)
    for k, hit in enumerate(hits):
        pos = jnp.where(lane == k, jnp.sum(jnp.where(hit, slot, 0.0), axis=1, keepdims=True), pos)
    pos_ref[0] = pos.astype(I32)


def _positions(idx, base):
    b, t, _ = idx.shape
    tile = pl.BlockSpec((1, TM, LANES), lambda bi, i: (bi, i, 0))
    return pl.pallas_call(
        _pos_kernel,
        grid=(b, t // TM),
        in_specs=[tile, pl.BlockSpec((1, 1, 1, LANES), lambda bi, i: (bi, i, 0, 0))],
        out_specs=tile,
        out_shape=jax.ShapeDtypeStruct((b, t, LANES), I32),
        compiler_params=_cparams(("arbitrary", "arbitrary")),
        name="moe_pos",
    )(idx, base)


def _scatter_kernel(pos_ref, hp_ref, init_ref, xs_ref, sem):
    del init_ref
    for j in range(TOP_K * TM):
        pltpu.make_async_copy(hp_ref.at[pl.ds(j % TM, 1)], xs_ref.at[pl.ds(pos_ref[0, 0, j], 1)],
                              sem).start(priority=j % 2)
    for k in range(TOP_K):
        pltpu.make_async_copy(hp_ref, xs_ref.at[pl.ds(0, TM)], sem).wait()


def _tile_major(pos):
    nt = pos.shape[0] // TM
    return pos[:, :TOP_K].reshape(nt, TM, TOP_K).transpose(0, 2, 1).reshape(nt, 1, TOP_K * TM)


def _scatter_rows(pos3, hp, n_slots):
    n, half = hp.shape
    return pl.pallas_call(
        _scatter_kernel,
        grid=(n // TM,),
        in_specs=[pl.BlockSpec((1, 1, TOP_K * TM), lambda i: (i, 0, 0), memory_space=pltpu.SMEM),
                  pl.BlockSpec((TM, half), lambda i: (i, 0)),
                  pl.BlockSpec(memory_space=pl.ANY)],
        out_specs=pl.BlockSpec(memory_space=pl.ANY),
        out_shape=jax.ShapeDtypeStruct((n_slots, half), hp.dtype),
        scratch_shapes=[pltpu.SemaphoreType.DMA(())],
        input_output_aliases={2: 0},
        compiler_params=_cparams(("arbitrary",)),
        name="moe_scatter",
    )(pos3, hp, jnp.zeros((n_slots, half), hp.dtype))


def _swiglu_packed(p, w1, w3, w2):
    lo, hi = _unpack_bf16_pair(p)
    lo, hi = lo.astype(BF16), hi.astype(BF16)
    n = lo.shape[-1]

    def up(w):
        return (jnp.dot(lo, w[:n], preferred_element_type=F32) + jnp.dot(hi, w[n:], preferred_element_type=F32))

    a = up(w1)
    return jnp.dot((a * jax.nn.sigmoid(a) * up(w3)).astype(BF16), w2, preferred_element_type=F32)


def _ffn_kernel(be_ref, x_ref, w1_ref, w3_ref, w2_ref, y_ref):
    del be_ref
    y_ref[...] = _pack_bf16_pair(_swiglu_packed(x_ref[...], w1_ref[0], w3_ref[0], w2_ref[0]))


def _expert_ffn(blk_e, xs, w1, w3, w2):
    n_slots, half = xs.shape
    e, d, ff = w1.shape
    row = pl.BlockSpec((EXPERT_BLOCK, half), lambda i, be: (i, 0))
    return pl.pallas_call(
        _ffn_kernel,
        grid_spec=pltpu.PrefetchScalarGridSpec(
            num_scalar_prefetch=1,
            grid=(n_slots // EXPERT_BLOCK,),
            in_specs=[row,
                      pl.BlockSpec((1, d, ff), lambda i, be: (be[i], 0, 0)),
                      pl.BlockSpec((1, d, ff), lambda i, be: (be[i], 0, 0)),
                      pl.BlockSpec((1, ff, d), lambda i, be: (be[i], 0, 0))],
            out_specs=row),
        out_shape=jax.ShapeDtypeStruct((n_slots, half), I32),
        compiler_params=_cparams(("arbitrary",)),
        name="moe_ffn",
    )(blk_e, xs, w1, w3, w2)


def _combine_kernel(pos_ref, gw_ref, ys_ref, hp_ref, x_ref, gt2, ws1, ws3, ws2, xo_ref, buf, sem):
    for j in range(TOP_K * TM):
        pltpu.make_async_copy(ys_ref.at[pl.ds(pos_ref[0, 0, j], 1)], buf.at[j // TM, pl.ds(j % TM, 1)],
                              sem).start(priority=j % 2)

    shared = _swiglu_packed(hp_ref[...], ws1[...], ws3[...], ws2[...])

    for k in range(TOP_K):
        pltpu.make_async_copy(ys_ref.at[pl.ds(0, TM)], buf.at[k], sem).wait()

    gw = gw_ref[...]
    lo = hi = None
    for k in range(TOP_K):
        wk = gw[:, k:k + 1]
        lk, hk = _unpack_bf16_pair(buf[k])
        lo = wk * lk if lo is None else lo + wk * lk
        hi = wk * hk if hi is None else hi + wk * hk
    xo_ref[...] = x_ref[...] + gt2[0] * (shared + jnp.concatenate([lo, hi], axis=1))


def _combine(pos3, gw, ys, hp, x_flat, gt2, ws1, ws3, ws2, tiles_per_batch, n_lat):
    n, d = x_flat.shape
    nt = n // TM

    def full(a):
        return pl.BlockSpec(a.shape, lambda i: (0,) * a.ndim)

    def tile(w):
        return pl.BlockSpec((TM, w), lambda i: (i, 0))

    def mod_idx(i):
        return (i // tiles_per_batch) * 2 + ((i % tiles_per_batch) >= n_lat).astype(I32)

    return pl.pallas_call(
        _combine_kernel,
        grid=(nt,),
        in_specs=[pl.BlockSpec((1, 1, TOP_K * TM), lambda i: (i, 0, 0), memory_space=pltpu.SMEM),
                  tile(LANES), pl.BlockSpec(memory_space=pl.ANY), tile(d // 2), tile(d),
                  pl.BlockSpec((1, 1, d), lambda i: (mod_idx(i), 0, 0)), full(ws1), full(ws3), full(ws2)],
        out_specs=tile(d),
        out_shape=jax.ShapeDtypeStruct((n, d), F32),
        scratch_shapes=[pltpu.VMEM((TOP_K, TM, d // 2), I32), pltpu.SemaphoreType.DMA(())],
        compiler_params=_cparams(("arbitrary",), 40),
        name="moe_combine",
    )(pos3, gw, ys, hp, x_flat, gt2, ws1, ws3, ws2)


def _final_kernel(x_ref, g_ref, o_ref):
    o_ref[0] = _rms(x_ref[0], g_ref[...])


def _final_norm(x_all, g, s_lat):
    b, _, d = x_all.shape
    return pl.pallas_call(
        _final_kernel,
        grid=(b, s_lat // TM),
        in_specs=[pl.BlockSpec((1, TM, d), lambda bi, i: (bi, i, 0)), pl.BlockSpec((1, d), lambda bi, i: (0, 0))],
        out_specs=pl.BlockSpec((1, TM, d), lambda bi, i: (bi, i, 0)),
        out_shape=jax.ShapeDtypeStruct((b, s_lat, d), F32),
        compiler_params=_cparams(("arbitrary", "arbitrary")),
        name="final_norm",
    )(x_all, g.reshape(1, d))


def _rope_tables(s_lat, t):
    pos = jnp.arange(s_lat, dtype=I32)
    half = 8
    freqs = ROPE_BASE ** (-jnp.arange(half, dtype=F32) * 2.0 / (2 * half))

    def part(p):
        ang = p.astype(F32)[:, None] * freqs[None, :]
        c, s = jnp.cos(ang), jnp.sin(ang)
        return jnp.concatenate([c, c], axis=1), jnp.concatenate([-s, s], axis=1)

    cr, sr = part(pos // GRID_W)
    cc, sc = part(pos % GRID_W)
    cos = jnp.tile(jnp.concatenate([cr, cc], axis=1), (1, 8))
    sin = jnp.tile(jnp.concatenate([sr, sc], axis=1), (1, 8))
    pad = t - s_lat
    return (jnp.concatenate([cos, jnp.ones((pad, 256), F32)], axis=0),
            jnp.concatenate([sin, jnp.zeros((pad, 256), F32)], axis=0))


def _heads(x, h):
    b, t, w = x.shape
    return x.reshape(b, t, h, w // h).transpose(0, 2, 1, 3)


def _tokens(o):
    b, h, t, d = o.shape
    return o.transpose(0, 2, 1, 3).reshape(b, t, h * d)


def kernel(x, c, ctx, c_ctx, w_ada, b_ada, g_norm1, g_norm2, w_in, na_rpb, diff_lambda, g_diff, g_mla_q, g_mla_kv, w_mla_uq, w_mla_ukv, s5_lam_re, s5_lam_im, s5_log_step, s5_b_re, s5_b_im, s5_c_re, s5_c_im, s5_d, w_glu, b_glu, w_branch, w_out, w_router, e_bias, w_e1, w_e3, w_e2, w_s1, w_s3, w_s2, g_final):
    b, s_lat, d = x.shape
    lc = ctx.shape[1]
    t = s_lat + lc
    depth = w_ada.shape[0]
    n_lat = s_lat // TM
    assert lc == TM and s_lat % TM == 0 and n_lat >= 3 and b + 1 <= 8 and d % 256 == 0
    tiles_per_batch = t // TM

    x_all = jnp.concatenate([x, ctx], axis=1)
    c8 = jnp.zeros((8, d), F32).at[:b].set(c).at[b].set(c_ctx)
    ada = _ada(c8, w_ada, b_ada)
    cos_t, sin_t = _rope_tables(s_lat, t)

    hq = jnp.arange(MLA_HEADS)[:, None] * (MLA_NOPE + MLA_ROPE)
    uq_cols = jnp.concatenate([(hq + jnp.arange(MLA_NOPE)[None]).reshape(-1),
                               (hq + MLA_NOPE + jnp.arange(MLA_ROPE)[None]).reshape(-1)])
    hk = jnp.arange(MLA_HEADS)[:, None] * (MLA_NOPE + MLA_V)
    ukv_cols = jnp.concatenate([(hk + jnp.arange(MLA_NOPE)[None]).reshape(-1),
                                (hk + MLA_NOPE + jnp.arange(MLA_V)[None]).reshape(-1)])
    kpe_end = 1952

    for l in range(depth):
        mods = jnp.concatenate([ada[l, :b, None, :], jnp.broadcast_to(ada[l, b], (b, 1, 6 * d))], axis=1)
        sh1, sc1, gt1, sh2, sc2, gt2 = (mods[:, :, i * d:(i + 1) * d].reshape(2 * b, 1, d) for i in range(6))
        w_main = jnp.concatenate([w_in[l][:, :kpe_end], jnp.zeros((d, OFF_U - kpe_end), F32),
                                  w_in[l][:, kpe_end:]], axis=1).astype(BF16)
        (naq, nak, nav, dfq, dfqr, dfk, dfkr, dfv, mlqn, mlqp, mlqpr, mlkn, mlv, mlkx, s5u, gates) = _proj(
            x_all, g_norm1[l].reshape(1, d), sh1, sc1, w_main, cos_t, sin_t, g_mla_q[l].reshape(1, -1),
            g_mla_kv[l].reshape(1, -1), w_mla_uq[l][:, uq_cols].astype(BF16), w_mla_ukv[l][:, ukv_cols].astype(BF16),
            n_lat)

        ya = _na(naq, nak, nav, _na_bias(na_rpb[l]), n_lat)

        lam_init = 0.8 - 0.6 * math.exp(-0.3 * l)
        dq = jnp.concatenate([_heads(dfqr, 2 * DIFF_HEADS), _heads(dfq, 2 * DIFF_HEADS)], axis=-1)
        dq = dq.reshape(b, DIFF_HEADS, 2, t, 2 * DIFF_DK).transpose(0, 1, 3, 2, 4).reshape(b, DIFF_HEADS, t, -1)
        dk = jnp.concatenate([_heads(dfkr, 2 * DIFF_HEADS), _heads(dfk, 2 * DIFF_HEADS)], axis=-1)
        dkt = dk.transpose(0, 1, 3, 2).reshape(b, DIFF_HEADS, 4 * DIFF_DK, t)
        yb = _tokens(_global_attention(
            functools.partial(_diff_kernel, lam_init=lam_init), "diff_attn", 2, dq, dkt, _heads(dfv, DIFF_HEADS),
            (diff_lambda[l], g_diff[l].reshape(1, -1)), s_lat))

        mq = jnp.concatenate([_heads(mlqn, MLA_HEADS), _heads(mlqpr, MLA_HEADS), _heads(mlqp, MLA_HEADS)], axis=-1)
        kx = jnp.broadcast_to(mlkx[:, None, :, :2 * MLA_ROPE], (b, MLA_HEADS, t, 2 * MLA_ROPE))
        mkt = jnp.concatenate([_heads(mlkn, MLA_HEADS), kx], axis=-1).transpose(0, 1, 3, 2)
        ym = _tokens(_global_attention(_mla_kernel, "mla_attn", 1, mq, mkt, _heads(mlv, MLA_HEADS), (), s_lat))

        ys = _s5_mixer(s5u, _s5_tables(s5_lam_re[l], s5_lam_im[l], s5_log_step[l], s5_b_re[l], s5_b_im[l],
                                       s5_c_re[l], s5_c_im[l]), s_lat)

        wr = jnp.concatenate([w_router[l], jnp.zeros((d, LANES - N_EXPERTS), F32)], axis=1)
        eb = jnp.concatenate([e_bias[l].astype(F32), jnp.full((LANES - N_EXPERTS,), -jnp.inf, F32)]).reshape(1, LANES)
        x_all, hp, idx, gw, hist = _merge(
            ya, yb, ym, ys, s5u, s5_d[l].reshape(1, -1), w_glu[l], b_glu[l].reshape(1, -1), gates,
            w_branch[l].astype(BF16), w_out[l].astype(BF16), x_all, gt1, g_norm2[l].reshape(1, d), sh2, sc2, wr, eb,
            n_lat)

        n = b * t
        base, blk_e, n_blocks = _route_tables(hist, n * TOP_K)
        pos3 = _tile_major(_positions(idx, base).reshape(n, LANES))
        hp_flat = hp.reshape(n, d // 2)
        xs = _scatter_rows(pos3, hp_flat, n_blocks * EXPERT_BLOCK)
        y_sorted = _expert_ffn(blk_e, xs, w_e1[l].astype(BF16), w_e3[l].astype(BF16), w_e2[l].astype(BF16))
        x_all = _combine(pos3, gw.reshape(n, LANES), y_sorted, hp_flat, x_all.reshape(n, d), gt2,
                         w_s1[l].astype(BF16), w_s3[l].astype(BF16), w_s2[l].astype(BF16),
                         tiles_per_batch, n_lat).reshape(b, t, d)

    return _final_norm(x_all, g_final, s_lat)
```
